```python
import jax, jax.numpy as jnp
from jax import lax
import numpy as np

D_MODEL = 2048
BATCH = 8
SEQ = 2048
DEPTH = 2

GRID_W = 64
CTX_LEN = 256
NA_HEADS = 8
HEAD_DIM = 128
NA_WIDTH = NA_HEADS * HEAD_DIM
CONV_CH = D_MODEL - NA_WIDTH
CONV_K = 3
WIN_ROWS = 8
WIN_COLS = 16
EVEN_IN = 3 * NA_WIDTH + 3 * CONV_CH
EVEN_SPLITS = (NA_WIDTH, 2 * NA_WIDTH, 3 * NA_WIDTH, 3 * NA_WIDTH + CONV_CH, 3 * NA_WIDTH + 2 * CONV_CH)
CHUNK = 128
SG_WIDTH = D_MODEL
SG_GROUPS = 16
SG_GROUP_DIM = SG_WIDTH // SG_GROUPS
FFN_HIDDEN = ((8 * D_MODEL // 3 + 255) // 256) * 256
EPS = 1e-6
NEG_INF = -1e30

kernel_name = "hybrid_natten_shortconv_gmlp_dit"


def rmsnorm(x, g):
    xf = x.astype(jnp.float32)
    y = xf * lax.rsqrt(jnp.mean(xf * xf, axis=-1, keepdims=True) + EPS)
    return (y * g.astype(jnp.float32)).astype(x.dtype)


def layernorm(x, g, b):
    xf = x.astype(jnp.float32)
    mu = jnp.mean(xf, axis=-1, keepdims=True)
    var = jnp.mean(jnp.square(xf - mu), axis=-1, keepdims=True)
    y = (xf - mu) * lax.rsqrt(var + EPS)
    return (y * g.astype(jnp.float32) + b.astype(jnp.float32)).astype(x.dtype)


def ada_mods(cond, w, b):
    m = jax.nn.silu(cond) @ w + b
    return jnp.split(m[:, None, :], 6, axis=-1)


def modulate(h, shift, scale):
    return h * (1.0 + scale) + shift


def heads(t):
    b, l, _ = t.shape
    return t.reshape(b, l, NA_HEADS, HEAD_DIM).transpose(0, 2, 1, 3)


def merge_heads(t):
    b, h, l, d = t.shape
    return t.transpose(0, 2, 1, 3).reshape(b, l, h * d)


def neighbourhood_attention(q, k, v, k_ctx, v_ctx, rpb, rows):
    b, h, s, d = q.shape
    kr = min(WIN_ROWS, rows)
    r = jnp.arange(rows)
    row_start = jnp.clip(r - kr // 2, 0, rows - kr)
    row_idx = row_start[:, None] + jnp.arange(kr)[None, :]
    col = jnp.arange(GRID_W)
    col_start = jnp.clip(col - WIN_COLS // 2, 0, GRID_W - WIN_COLS)
    col_mask = (col[None, :] >= col_start[:, None]) & (col[None, :] < col_start[:, None] + WIN_COLS)
    mask = jnp.tile(col_mask, (1, kr))

    qg = q.reshape(b, h, rows, GRID_W, d)
    kg = k.reshape(b, h, rows, GRID_W, d)[:, :, row_idx].reshape(b, h, rows, kr * GRID_W, d)
    vg = v.reshape(b, h, rows, GRID_W, d)[:, :, row_idx].reshape(b, h, rows, kr * GRID_W, d)

    dr = row_idx - r[:, None] + (WIN_ROWS - 1)
    dc = jnp.clip(col[None, :] - col[:, None], -(WIN_COLS - 1), WIN_COLS - 1) + (WIN_COLS - 1)
    bias = rpb[:, dr[:, None, :, None], dc[None, :, None, :]]
    bias = bias.reshape(h, rows, GRID_W, kr * GRID_W).astype(jnp.float32)

    scale = d ** -0.5
    s_loc = jnp.einsum('bhrqd,bhrkd->bhrqk', qg, kg).astype(jnp.float32) * scale + bias
    s_loc = jnp.where(mask, s_loc, NEG_INF)
    s_ctx = jnp.einsum('bhrqd,bhld->bhrql', qg, k_ctx).astype(jnp.float32) * scale
    p = jax.nn.softmax(jnp.concatenate([s_loc, s_ctx], axis=-1), axis=-1).astype(v.dtype)
    p_loc, p_ctx = p[..., :kr * GRID_W], p[..., kr * GRID_W:]
    out = (jnp.einsum('bhrqk,bhrkd->bhrqd', p_loc, vg)
           + jnp.einsum('bhrql,bhld->bhrqd', p_ctx, v_ctx))
    return out.reshape(b, h, s, d)


def context_attention(q, k, v):
    s = jnp.einsum('bhqd,bhkd->bhqk', q, k).astype(jnp.float32) * (q.shape[-1] ** -0.5)
    p = jax.nn.softmax(s, axis=-1).astype(v.dtype)
    return jnp.einsum('bhqk,bhkd->bhqd', p, v)


def short_gated_conv(u, gate_b, gate_c, w_conv):
    z = gate_c * u
    zp = jnp.pad(z, ((0, 0), (1, 1), (0, 0)))
    y = zp[:, :-2] * w_conv[0] + zp[:, 1:-1] * w_conv[1] + zp[:, 2:] * w_conv[2]
    return gate_b * y


def even_mixer(hx, hc, w_in, w_out, rpb, w_conv, rows, ctx_out):
    qx, kx, vx, bx, cx, ux = jnp.split(hx @ w_in, EVEN_SPLITS, axis=-1)
    if ctx_out:
        qc, kc, vc, bc, cc, uc = jnp.split(hc @ w_in, EVEN_SPLITS, axis=-1)
    else:
        kc, vc = jnp.split(hc @ w_in[:, NA_WIDTH:3 * NA_WIDTH], 2, axis=-1)
    kc_h, vc_h = heads(kc), heads(vc)
    attn_x = neighbourhood_attention(heads(qx), heads(kx), heads(vx), kc_h, vc_h, rpb, rows)
    mix_x = jnp.concatenate([merge_heads(attn_x), short_gated_conv(ux, bx, cx, w_conv)], axis=-1) @ w_out
    if not ctx_out:
        return mix_x, None
    attn_c = context_attention(heads(qc), kc_h, vc_h)
    mix_c = jnp.concatenate([merge_heads(attn_c), short_gated_conv(uc, bc, cc, w_conv)], axis=-1) @ w_out
    return mix_x, mix_c


def odd_mixer(h, w_in, w_out, ln_g, ln_b, w_s, b_s):
    z = jax.nn.gelu(h @ w_in)
    u, v = jnp.split(z, 2, axis=-1)
    v = layernorm(v, ln_g, ln_b)
    b, l, _ = v.shape
    n = l // CHUNK
    vg = v.reshape(b, n, CHUNK, SG_GROUPS, SG_GROUP_DIM)
    mixed = jnp.einsum('gts,bnsgc->bntgc', w_s, vg) + b_s.T[None, None, :, :, None]
    return (u * mixed.reshape(b, l, SG_WIDTH)) @ w_out


def swiglu(h, w_gate, w_up, w_down):
    return (jax.nn.silu(h @ w_gate) * (h @ w_up)) @ w_down


def setup_inputs(seed: int = 0) -> dict:
    key = jax.random.key(seed)
    ks = jax.random.split(key, 24)
    n_even = (DEPTH + 1) // 2
    n_odd = DEPTH // 2
    f32 = jnp.float32

    def nrm(k, shape, scale):
        return jax.random.normal(k, shape, f32) * scale

    return {
        "x": nrm(ks[0], (BATCH, SEQ, D_MODEL), 1.0),
        "c": nrm(ks[1], (BATCH, D_MODEL), 1.0),
        "ctx": nrm(ks[2], (BATCH, CTX_LEN, D_MODEL), 1.0),
        "c_ctx": nrm(ks[3], (D_MODEL,), 1.0),
        "ada_w": nrm(ks[4], (DEPTH, D_MODEL, 6 * D_MODEL), 0.5 * D_MODEL ** -0.5),
        "ada_b": nrm(ks[5], (DEPTH, 6 * D_MODEL), 0.02),
        "norm_mix_g": 1.0 + nrm(ks[6], (DEPTH, D_MODEL), 0.02),
        "norm_ffn_g": 1.0 + nrm(ks[7], (DEPTH, D_MODEL), 0.02),
        "ffn_w_gate": nrm(ks[8], (DEPTH, D_MODEL, FFN_HIDDEN), D_MODEL ** -0.5),
        "ffn_w_up": nrm(ks[9], (DEPTH, D_MODEL, FFN_HIDDEN), D_MODEL ** -0.5),
        "ffn_w_down": nrm(ks[10], (DEPTH, FFN_HIDDEN, D_MODEL), FFN_HIDDEN ** -0.5),
        "even_w_in": nrm(ks[11], (n_even, D_MODEL, EVEN_IN), D_MODEL ** -0.5),
        "even_w_out": nrm(ks[12], (n_even, NA_WIDTH + CONV_CH, D_MODEL), (NA_WIDTH + CONV_CH) ** -0.5),
        "rpb": nrm(ks[13], (n_even, NA_HEADS, 2 * WIN_ROWS - 1, 2 * WIN_COLS - 1), 0.1),
        "conv_w": nrm(ks[14], (n_even, CONV_K, CONV_CH), CONV_K ** -0.5),
        "odd_w_in": nrm(ks[15], (n_odd, D_MODEL, 2 * SG_WIDTH), D_MODEL ** -0.5),
        "odd_w_out": nrm(ks[16], (n_odd, SG_WIDTH, D_MODEL), SG_WIDTH ** -0.5),
        "sg_ln_g": 1.0 + nrm(ks[17], (n_odd, SG_WIDTH), 0.02),
        "sg_ln_b": nrm(ks[18], (n_odd, SG_WIDTH), 0.02),
        "sg_w": nrm(ks[19], (n_odd, SG_GROUPS, CHUNK, CHUNK), CHUNK ** -0.5),
        "sg_b": 1.0 + nrm(ks[20], (n_odd, SG_GROUPS, CHUNK), 0.02),
        "final_norm_g": 1.0 + nrm(ks[21], (D_MODEL,), 0.02),
    }


def reference(x, c, ctx, c_ctx, ada_w, ada_b, norm_mix_g, norm_ffn_g, ffn_w_gate, ffn_w_up,
              ffn_w_down, even_w_in, even_w_out, rpb, conv_w, odd_w_in, odd_w_out,
              sg_ln_g, sg_ln_b, sg_w, sg_b, final_norm_g):
    rows = x.shape[1] // GRID_W
    for i in range(DEPTH):
        j = i // 2
        ctx_needed = any(l % 2 == 0 for l in range(i + 1, DEPTH))
        sh_x, sc_x, g_x, shf_x, scf_x, gf_x = ada_mods(c, ada_w[i], ada_b[i])
        sh_c, sc_c, g_c, shf_c, scf_c, gf_c = ada_mods(c_ctx[None], ada_w[i], ada_b[i])

        hx = modulate(rmsnorm(x, norm_mix_g[i]), sh_x, sc_x)
        need_hc = (i % 2 == 0) or ctx_needed
        hc = modulate(rmsnorm(ctx, norm_mix_g[i]), sh_c, sc_c) if need_hc else None

        if i % 2 == 0:
            mix_x, mix_c = even_mixer(hx, hc, even_w_in[j], even_w_out[j], rpb[j], conv_w[j],
                                      rows, ctx_needed)
        else:
            mix_x = odd_mixer(hx, odd_w_in[j], odd_w_out[j], sg_ln_g[j], sg_ln_b[j], sg_w[j], sg_b[j])
            mix_c = (odd_mixer(hc, odd_w_in[j], odd_w_out[j], sg_ln_g[j], sg_ln_b[j], sg_w[j], sg_b[j])
                     if ctx_needed else None)

        x = x + g_x * mix_x
        hx = modulate(rmsnorm(x, norm_ffn_g[i]), shf_x, scf_x)
        x = x + gf_x * swiglu(hx, ffn_w_gate[i], ffn_w_up[i], ffn_w_down[i])

        if ctx_needed:
            ctx = ctx + g_c * mix_c
            hc = modulate(rmsnorm(ctx, norm_ffn_g[i]), shf_c, scf_c)
            ctx = ctx + gf_c * swiglu(hc, ffn_w_gate[i], ffn_w_up[i], ffn_w_down[i])

    return rmsnorm(x, final_norm_g)
```

```python
import functools

import jax
import jax.numpy as jnp
from jax import lax
from jax.experimental import pallas as pl
from jax.experimental.pallas import tpu as pltpu

D_MODEL = 2048
GRID_W = 64
NA_HEADS = 8
HEAD_DIM = 128
NA_WIDTH = NA_HEADS * HEAD_DIM
CONV_CH = D_MODEL - NA_WIDTH
WIN_ROWS = 8
WIN_COLS = 16
CHUNK = 128
SG_GROUPS = 16
EPS = 1e-6
NEG_INF = -1e30
UNMASKED_CAP = 3e38
MOD_ROWS = 16
N_MODS = 6

F32 = jnp.float32
BF16 = jnp.bfloat16

VMEM_LIMIT_BYTES = 56 * 1024 * 1024


def _params(n_axes):
    return pltpu.CompilerParams(dimension_semantics=("arbitrary",) * n_axes,
                                vmem_limit_bytes=VMEM_LIMIT_BYTES)


def _dot(a, b):
    return jnp.dot(a, b, preferred_element_type=F32)


def _norm_mod(x, g, shift, scale):
    y = x * lax.rsqrt(jnp.mean(x * x, axis=-1, keepdims=True) + EPS)
    return (y * g) * (1.0 + scale) + shift


def _ada_kernel(cond_ref, w_ref, b_ref, o_ref):
    a = jax.nn.silu(cond_ref[...]).astype(BF16)
    o_ref[...] = _dot(a, w_ref[...].astype(BF16)) + b_ref[...]


def _ada_mods(cond, ada_w, ada_b):
    depth, _, n = ada_w.shape
    tn = 1024
    return pl.pallas_call(
        _ada_kernel,
        grid=(depth, n // tn),
        in_specs=[
            pl.BlockSpec((MOD_ROWS, D_MODEL), lambda l, j: (0, 0)),
            pl.BlockSpec((None, D_MODEL, tn), lambda l, j: (l, 0, j)),
            pl.BlockSpec((None, 1, tn), lambda l, j: (l, 0, j)),
        ],
        out_specs=pl.BlockSpec((None, MOD_ROWS, tn), lambda l, j: (l, 0, j)),
        out_shape=jax.ShapeDtypeStruct((depth, MOD_ROWS, n), F32),
        compiler_params=_params(2),
        name="ada_mods",
    )(cond, ada_w, ada_b.reshape(depth, 1, n))


def _mod_spec(layer, chunk, n_grid_axes):
    if n_grid_axes == 1:
        return pl.BlockSpec((None, MOD_ROWS, D_MODEL), lambda i: (layer, 0, chunk))
    return pl.BlockSpec((None, MOD_ROWS, D_MODEL), lambda i, j: (layer, 0, chunk))


def _norm_matmul_kernel(x_ref, g_ref, sh_ref, sc_ref, w_ref, o_ref, h_scr, *, tiles_per_mod, mod_row0):
    i = pl.program_id(0)

    @pl.when(pl.program_id(1) == 0)
    def _():
        row = mod_row0 + i // tiles_per_mod
        h = _norm_mod(x_ref[...], g_ref[...], sh_ref[pl.ds(row, 1), :], sc_ref[pl.ds(row, 1), :])
        h_scr[...] = h.astype(BF16)

    o_ref[...] = _dot(h_scr[...], w_ref[...]).astype(o_ref.dtype)


def _norm_matmul(x2d, g, mods, layer, w, *, col0, n_cols, rows_per_mod, mod_row0, tm, tn, name):
    m = x2d.shape[0]
    tiles_per_mod = max(rows_per_mod // tm, 1)
    kern = functools.partial(_norm_matmul_kernel, tiles_per_mod=tiles_per_mod, mod_row0=mod_row0)
    cb0 = col0 // tn
    return pl.pallas_call(
        kern,
        grid=(m // tm, n_cols // tn),
        in_specs=[
            pl.BlockSpec((tm, D_MODEL), lambda i, j: (i, 0)),
            pl.BlockSpec((None, 1, D_MODEL), lambda i, j: (layer, 0, 0)),
            _mod_spec(layer, 0, 2),
            _mod_spec(layer, 1, 2),
            pl.BlockSpec((D_MODEL, tn), lambda i, j: (0, cb0 + j)),
        ],
        out_specs=pl.BlockSpec((tm, tn), lambda i, j: (i, j)),
        out_shape=jax.ShapeDtypeStruct((m, n_cols), BF16),
        scratch_shapes=[pltpu.VMEM((tm, D_MODEL), BF16)],
        compiler_params=_params(2),
        name=name,
    )(x2d, g, mods, mods, w)


def _bias_kernel(rpb_ref, bias_ref, cap_ref):
    h = pl.program_id(0)
    n_dc = 2 * WIN_COLS - 1
    n_dr = 2 * WIN_ROWS - 1
    q = lax.broadcasted_iota(jnp.int32, (GRID_W, 2 * GRID_W), 0)
    lane = lax.broadcasted_iota(jnp.int32, (GRID_W, 2 * GRID_W), 1)
    second = lane >= GRID_W
    k = jnp.where(second, lane - GRID_W, lane)
    dc = jnp.clip(k - q, -(WIN_COLS - 1), WIN_COLS - 1) + (WIN_COLS - 1)
    pairs = []
    for m in range(n_dr - 1):
        acc = jnp.zeros((GRID_W, 2 * GRID_W), F32)
        for d in range(n_dc):
            val = jnp.where(second, rpb_ref[h, (m + 1) * n_dc + d], rpb_ref[h, m * n_dc + d])
            acc = jnp.where(dc == d, val, acc)
        pairs.append(acc)
    for o in range(WIN_ROWS):
        for jj in range(WIN_ROWS // 2):
            bias_ref[o, :, jj * 2 * GRID_W:(jj + 1) * 2 * GRID_W] = pairs[o + 2 * jj]
    col_start = jnp.clip(q - WIN_COLS // 2, 0, GRID_W - WIN_COLS)
    cap = jnp.where((k >= col_start) & (k < col_start + WIN_COLS), UNMASKED_CAP, NEG_INF).astype(F32)
    for jj in range(WIN_ROWS // 2):
        cap_ref[:, jj * 2 * GRID_W:(jj + 1) * 2 * GRID_W] = cap


def _bias_tables(rpb):
    h = rpb.shape[0]
    kw = WIN_ROWS * GRID_W
    return pl.pallas_call(
        _bias_kernel,
        grid=(h,),
        in_specs=[pl.BlockSpec(memory_space=pltpu.SMEM)],
        out_specs=[
            pl.BlockSpec((None, WIN_ROWS, GRID_W, kw), lambda i: (i, 0, 0, 0)),
            pl.BlockSpec((GRID_W, kw), lambda i: (0, 0)),
        ],
        out_shape=[
            jax.ShapeDtypeStruct((h, WIN_ROWS, GRID_W, kw), F32),
            jax.ShapeDtypeStruct((GRID_W, kw), F32),
        ],
        compiler_params=_params(1),
        name="bias_tables",
    )(rpb.reshape(h, -1))


def _attn_kernel(q_ref, k_ref, v_ref, kc_ref, vc_ref, bias_ref, cap_ref, o_ref, *, rows):
    scale = HEAD_DIM ** -0.5
    kw = WIN_ROWS * GRID_W
    nt = (((1,), (1,)), ((), ()))
    kc = kc_ref[...]
    vc = vc_ref[...]

    def body(r, carry):
        row_start = jnp.clip(r - WIN_ROWS // 2, 0, rows - WIN_ROWS)
        o = row_start - r + (WIN_ROWS - 1)
        q0 = pl.multiple_of(r * GRID_W, GRID_W)
        k0 = pl.multiple_of(row_start * GRID_W, GRID_W)
        q = q_ref[pl.ds(q0, GRID_W), :]
        kwin = k_ref[pl.ds(k0, kw), :]
        vwin = v_ref[pl.ds(k0, kw), :]
        s_loc = lax.dot_general(q, kwin, nt, preferred_element_type=F32) * scale + bias_ref[o]
        s_loc = jnp.minimum(s_loc, cap_ref[...])
        s_ctx = lax.dot_general(q, kc, nt, preferred_element_type=F32) * scale
        m = jnp.maximum(jnp.max(s_loc, axis=-1, keepdims=True), jnp.max(s_ctx, axis=-1, keepdims=True))
        e_loc = jnp.exp(s_loc - m)
        e_ctx = jnp.exp(s_ctx - m)
        inv = 1.0 / (jnp.sum(e_loc, axis=-1, keepdims=True) + jnp.sum(e_ctx, axis=-1, keepdims=True))
        out = _dot((e_loc * inv).astype(BF16), vwin) + _dot((e_ctx * inv).astype(BF16), vc)
        o_ref[pl.ds(q0, GRID_W), :] = out.astype(o_ref.dtype)
        return carry

    lax.fori_loop(0, rows, body, 0)


def _attention(qkv, ctx_kv, bias, cap, batch, seq, ctx_len):
    rows = seq // GRID_W
    kw = WIN_ROWS * GRID_W
    kern = functools.partial(_attn_kernel, rows=rows)
    return pl.pallas_call(
        kern,
        grid=(NA_HEADS, batch),
        in_specs=[
            pl.BlockSpec((seq, HEAD_DIM), lambda h, b: (b, h)),
            pl.BlockSpec((seq, HEAD_DIM), lambda h, b: (b, NA_HEADS + h)),
            pl.BlockSpec((seq, HEAD_DIM), lambda h, b: (b, 2 * NA_HEADS + h)),
            pl.BlockSpec((ctx_len, HEAD_DIM), lambda h, b: (b, h)),
            pl.BlockSpec((ctx_len, HEAD_DIM), lambda h, b: (b, NA_HEADS + h)),
            pl.BlockSpec((None, WIN_ROWS, GRID_W, kw), lambda h, b: (h, 0, 0, 0)),
            pl.BlockSpec((GRID_W, kw), lambda h, b: (0, 0)),
        ],
        out_specs=pl.BlockSpec((seq, HEAD_DIM), lambda h, b: (b, h)),
        out_shape=jax.ShapeDtypeStruct((batch * seq, NA_WIDTH), BF16),
        compiler_params=_params(2),
        name="nbr_attention",
    )(qkv, qkv, qkv, ctx_kv, ctx_kv, bias, cap)


def _residual_and_ffn_norm(i, tiles_per_mod, x_ref, mix, gate_ref, gn_ref, shf_ref, scf_ref, x1_ref, h_ref):
    row = i // tiles_per_mod
    x1 = x_ref[...] + gate_ref[pl.ds(row, 1), :] * mix
    x1_ref[...] = x1
    h = _norm_mod(x1, gn_ref[...], shf_ref[pl.ds(row, 1), :], scf_ref[pl.ds(row, 1), :])
    h_ref[...] = h.astype(BF16)


def _even_out_kernel(attn_ref, b_ref, c_ref, u_ref, cp_ref, up_ref, cn_ref, un_ref, cw_ref, wo_ref,
                     x_ref, gate_ref, gn_ref, shf_ref, scf_ref, x1_ref, h_ref, *, tiles_per_mod, halo):
    i = pl.program_id(0)
    tm = c_ref.shape[0]
    t = i % tiles_per_mod
    z = c_ref[...].astype(F32) * u_ref[...].astype(F32)
    z_before = cp_ref[halo - 1:halo, :].astype(F32) * up_ref[halo - 1:halo, :].astype(F32)
    z_after = cn_ref[0:1, :].astype(F32) * un_ref[0:1, :].astype(F32)
    z_before = jnp.where(t == 0, 0.0, z_before)
    z_after = jnp.where(t == tiles_per_mod - 1, 0.0, z_after)
    ridx = lax.broadcasted_iota(jnp.int32, (tm, 1), 0)
    z_prev = jnp.where(ridx == 0, z_before, pltpu.roll(z, 1, 0))
    z_next = jnp.where(ridx == tm - 1, z_after, pltpu.roll(z, tm - 1, 0))
    y = z_prev * cw_ref[0:1, :] + z * cw_ref[1:2, :] + z_next * cw_ref[2:3, :]
    conv = (b_ref[...].astype(F32) * y).astype(BF16)
    mix = _dot(attn_ref[...], wo_ref[0:NA_WIDTH, :]) + _dot(conv, wo_ref[NA_WIDTH:, :])
    _residual_and_ffn_norm(i, tiles_per_mod, x_ref, mix, gate_ref, gn_ref, shf_ref, scf_ref, x1_ref, h_ref)


def _even_out(attn, proj, conv_w, w_out, x2d, mods, norm_ffn_g, layer, seq, tm):
    m = x2d.shape[0]
    halo = 16
    hb = tm // halo
    n_halo_blocks = m // halo
    cb = NA_WIDTH * 3 // CONV_CH
    kern = functools.partial(_even_out_kernel, tiles_per_mod=seq // tm, halo=halo)
    main = lambda c: pl.BlockSpec((tm, CONV_CH), lambda i: (i, c))
    prev = lambda c: pl.BlockSpec((halo, CONV_CH), lambda i: (jnp.maximum(i * hb - 1, 0), c))
    nxt = lambda c: pl.BlockSpec((halo, CONV_CH), lambda i: (jnp.minimum((i + 1) * hb, n_halo_blocks - 1), c))
    return pl.pallas_call(
        kern,
        grid=(m // tm,),
        in_specs=[
            pl.BlockSpec((tm, NA_WIDTH), lambda i: (i, 0)),
            main(cb), main(cb + 1), main(cb + 2),
            prev(cb + 1), prev(cb + 2), nxt(cb + 1), nxt(cb + 2),
            pl.BlockSpec(conv_w.shape, lambda i: (0, 0)),
            pl.BlockSpec(w_out.shape, lambda i: (0, 0)),
            pl.BlockSpec((tm, D_MODEL), lambda i: (i, 0)),
            _mod_spec(layer, 2, 1),
            pl.BlockSpec((None, 1, D_MODEL), lambda i: (layer, 0, 0)),
            _mod_spec(layer, 3, 1),
            _mod_spec(layer, 4, 1),
        ],
        out_specs=[
            pl.BlockSpec((tm, D_MODEL), lambda i: (i, 0)),
            pl.BlockSpec((tm, D_MODEL), lambda i: (i, 0)),
        ],
        out_shape=[
            jax.ShapeDtypeStruct((m, D_MODEL), F32),
            jax.ShapeDtypeStruct((m, D_MODEL), BF16),
        ],
        compiler_params=_params(1),
        name="even_out",
    )(attn, proj, proj, proj, proj, proj, proj, proj, conv_w, w_out, x2d, mods, norm_ffn_g, mods, mods)


def _ffn_kernel(*refs, tiles_per_mod, modulated):
    if modulated:
        (h_ref, wg_ref, wu_ref, wd_ref, x_ref, gate_ref, gn_ref, sh_ref, sc_ref,
         x2_ref, y_ref, acc_ref) = refs
    else:
        h_ref, wg_ref, wu_ref, wd_ref, x_ref, gate_ref, gn_ref, y_ref, acc_ref = refs
    i = pl.program_id(0)
    f = pl.program_id(1)
    h = h_ref[...]
    t = (jax.nn.silu(_dot(h, wg_ref[...])) * _dot(h, wu_ref[...])).astype(BF16)
    part = _dot(t, wd_ref[...])

    @pl.when(f == 0)
    def _():
        acc_ref[...] = part

    @pl.when(f > 0)
    def _():
        acc_ref[...] += part

    @pl.when(f == pl.num_programs(1) - 1)
    def _():
        row = i // tiles_per_mod
        x2 = x_ref[...] + gate_ref[pl.ds(row, 1), :] * acc_ref[...]
        if modulated:
            x2_ref[...] = x2
            y = _norm_mod(x2, gn_ref[...], sh_ref[pl.ds(row, 1), :], sc_ref[pl.ds(row, 1), :])
        else:
            y = x2 * lax.rsqrt(jnp.mean(x2 * x2, axis=-1, keepdims=True) + EPS) * gn_ref[...]
        y_ref[...] = y.astype(y_ref.dtype)


def _ffn(h, w_gate, w_up, w_down, x2d, mods, layer, next_g, next_layer, seq, tm, tf):
    m = x2d.shape[0]
    hidden = w_gate.shape[1]
    modulated = next_layer is not None
    kern = functools.partial(_ffn_kernel, tiles_per_mod=seq // tm, modulated=modulated)
    row_spec = pl.BlockSpec((tm, D_MODEL), lambda i, f: (i, 0))
    in_specs = [
        row_spec,
        pl.BlockSpec((D_MODEL, tf), lambda i, f: (0, f)),
        pl.BlockSpec((D_MODEL, tf), lambda i, f: (0, f)),
        pl.BlockSpec((tf, D_MODEL), lambda i, f: (f, 0)),
        row_spec,
        _mod_spec(layer, 5, 2),
    ]
    args = [h, w_gate, w_up, w_down, x2d, mods]
    if modulated:
        in_specs += [pl.BlockSpec((None, 1, D_MODEL), lambda i, f: (next_layer, 0, 0)),
                     _mod_spec(next_layer, 0, 2), _mod_spec(next_layer, 1, 2)]
        args += [next_g, mods, mods]
        out_specs = [row_spec, row_spec]
        out_shape = [jax.ShapeDtypeStruct((m, D_MODEL), F32), jax.ShapeDtypeStruct((m, D_MODEL), BF16)]
    else:
        in_specs += [pl.BlockSpec((1, D_MODEL), lambda i, f: (0, 0))]
        args += [next_g]
        out_specs = row_spec
        out_shape = jax.ShapeDtypeStruct((m, D_MODEL), F32)
    return pl.pallas_call(
        kern,
        grid=(m // tm, hidden // tf),
        in_specs=in_specs,
        out_specs=out_specs,
        out_shape=out_shape,
        scratch_shapes=[pltpu.VMEM((tm, D_MODEL), F32)],
        compiler_params=_params(2),
        name="ffn_final" if not modulated else "ffn",
    )(*args)


def _odd_in_kernel(h_ref, w_ref, lg_ref, lb_ref, o_ref):
    z = jax.nn.gelu(_dot(h_ref[...], w_ref[...]), approximate=True)

    @pl.when(pl.program_id(1) == 0)
    def _():
        o_ref[...] = z.astype(o_ref.dtype)

    @pl.when(pl.program_id(1) == 1)
    def _():
        mu = jnp.mean(z, axis=-1, keepdims=True)
        zc = z - mu
        var = jnp.mean(zc * zc, axis=-1, keepdims=True)
        o_ref[...] = (zc * lax.rsqrt(var + EPS) * lg_ref[...] + lb_ref[...]).astype(o_ref.dtype)


def _odd_in(h, w_in, ln_g, ln_b, tm):
    m = h.shape[0]
    return pl.pallas_call(
        _odd_in_kernel,
        grid=(m // tm, 2),
        in_specs=[
            pl.BlockSpec((tm, D_MODEL), lambda i, j: (i, 0)),
            pl.BlockSpec((D_MODEL, D_MODEL), lambda i, j: (0, j)),
            pl.BlockSpec((1, D_MODEL), lambda i, j: (0, 0)),
            pl.BlockSpec((1, D_MODEL), lambda i, j: (0, 0)),
        ],
        out_specs=pl.BlockSpec((tm, D_MODEL), lambda i, j: (i, j)),
        out_shape=jax.ShapeDtypeStruct((m, 2 * D_MODEL), BF16),
        compiler_params=_params(2),
        name="odd_in",
    )(h, w_in, ln_g, ln_b)


def _odd_out_kernel(u_ref, v_ref, ws_ref, bs_ref, wo_ref, x_ref, gate_ref, gn_ref, shf_ref, scf_ref,
                    x1_ref, h_ref, t_scr, *, tiles_per_mod):
    i = pl.program_id(0)
    n_chunks = u_ref.shape[0] // CHUNK
    for g in range(SG_GROUPS):
        cols = slice(g * CHUNK, (g + 1) * CHUNK)
        vcat = jnp.concatenate([v_ref[n * CHUNK:(n + 1) * CHUNK, cols] for n in range(n_chunks)], axis=1)
        mixed = _dot(ws_ref[g], vcat)
        for n in range(n_chunks):
            rows = slice(n * CHUNK, (n + 1) * CHUNK)
            gated = u_ref[rows, cols].astype(F32) * (mixed[:, n * CHUNK:(n + 1) * CHUNK] + bs_ref[g])
            t_scr[rows, cols] = gated.astype(BF16)
    mix = _dot(t_scr[...], wo_ref[...])
    _residual_and_ffn_norm(i, tiles_per_mod, x_ref, mix, gate_ref, gn_ref, shf_ref, scf_ref, x1_ref, h_ref)


def _odd_out(uv, w_s, b_s, w_out, x2d, mods, norm_ffn_g, layer, seq, tm):
    m = x2d.shape[0]
    kern = functools.partial(_odd_out_kernel, tiles_per_mod=seq // tm)
    return pl.pallas_call(
        kern,
        grid=(m // tm,),
        in_specs=[
            pl.BlockSpec((tm, D_MODEL), lambda i: (i, 0)),
            pl.BlockSpec((tm, D_MODEL), lambda i: (i, 1)),
            pl.BlockSpec(w_s.shape, lambda i: (0, 0, 0)),
            pl.BlockSpec(b_s.shape, lambda i: (0, 0, 0)),
            pl.BlockSpec(w_out.shape, lambda i: (0, 0)),
            pl.BlockSpec((tm, D_MODEL), lambda i: (i, 0)),
            _mod_spec(layer, 2, 1),
            pl.BlockSpec((None, 1, D_MODEL), lambda i: (layer, 0, 0)),
            _mod_spec(layer, 3, 1),
            _mod_spec(layer, 4, 1),
        ],
        out_specs=[
            pl.BlockSpec((tm, D_MODEL), lambda i: (i, 0)),
            pl.BlockSpec((tm, D_MODEL), lambda i: (i, 0)),
        ],
        out_shape=[
            jax.ShapeDtypeStruct((m, D_MODEL), F32),
            jax.ShapeDtypeStruct((m, D_MODEL), BF16),
        ],
        scratch_shapes=[pltpu.VMEM((tm, D_MODEL), BF16)],
        compiler_params=_params(1),
        name="odd_out",
    )(uv, uv, w_s, b_s, w_out, x2d, mods, norm_ffn_g, mods, mods, )


def kernel(x, c, ctx, c_ctx, ada_w, ada_b, norm_mix_g, norm_ffn_g, ffn_w_gate, ffn_w_up, ffn_w_down,
           even_w_in, even_w_out, rpb, conv_w, odd_w_in, odd_w_out, sg_ln_g, sg_ln_b, sg_w, sg_b,
           final_norm_g):
    batch, seq, d = x.shape
    ctx_len = ctx.shape[1]
    depth = ada_w.shape[0]
    assert d == D_MODEL and depth == 2 and batch < MOD_ROWS and seq % (GRID_W * WIN_ROWS) == 0

    bf = lambda w: w.astype(BF16)
    x2d = x.reshape(batch * seq, d)
    ctx2d = ctx.reshape(batch * ctx_len, d)
    g_mix = norm_mix_g.reshape(depth, 1, d)
    g_ffn = norm_ffn_g.reshape(depth, 1, d)

    cond = jnp.concatenate([c, c_ctx[None], jnp.zeros((MOD_ROWS - batch - 1, d), F32)], axis=0)
    mods = _ada_mods(cond, ada_w, ada_b)

    w_in0 = bf(even_w_in[0])
    proj = _norm_matmul(x2d, g_mix, mods, 0, w_in0, col0=0, n_cols=w_in0.shape[1], rows_per_mod=seq,
                        mod_row0=0, tm=1024, tn=1024, name="even_in")
    ctx_kv = _norm_matmul(ctx2d, g_mix, mods, 0, w_in0, col0=NA_WIDTH, n_cols=2 * NA_WIDTH,
                          rows_per_mod=batch * ctx_len, mod_row0=batch, tm=1024, tn=1024, name="ctx_kv")
    bias, cap = _bias_tables(rpb[0])
    attn = _attention(proj, ctx_kv, bias, cap, batch, seq, ctx_len)
    x1, h = _even_out(attn, proj, conv_w[0], bf(even_w_out[0]), x2d, mods, g_ffn, 0, seq, tm=512)
    x2, h = _ffn(h, bf(ffn_w_gate[0]), bf(ffn_w_up[0]), bf(ffn_w_down[0]), x1, mods, 0,
                 g_mix, 1, seq, tm=512, tf=512)

    uv = _odd_in(h, bf(odd_w_in[0]), sg_ln_g[0][None], sg_ln_b[0][None], tm=512)
    b_s = jnp.broadcast_to(sg_b[0][:, :, None], (SG_GROUPS, CHUNK, CHUNK))
    x1, h = _odd_out(uv, bf(sg_w[0]), b_s, bf(odd_w_out[0]), x2, mods, g_ffn, 1, seq, tm=512)
    out = _ffn(h, bf(ffn_w_gate[1]), bf(ffn_w_up[1]), bf(ffn_w_down[1]), x1, mods, 1,
               final_norm_g[None], None, seq, tm=512, tf=512)
    return out.reshape(batch, seq, d)
```

```python
import functools

import jax
import jax.numpy as jnp
from jax import lax
from jax.experimental import pallas as pl
from jax.experimental.pallas import tpu as pltpu

D_MODEL = 2048
GRID_W = 64
NA_HEADS = 8
HEAD_DIM = 128
NA_WIDTH = NA_HEADS * HEAD_DIM
CONV_CH = D_MODEL - NA_WIDTH
WIN_ROWS = 8
WIN_COLS = 16
CHUNK = 128
SG_GROUPS = 16
EPS = 1e-6
NEG_INF = -1e30
UNMASKED_CAP = 3e38
MOD_ROWS = 16
N_MODS = 6

F32 = jnp.float32
BF16 = jnp.bfloat16

VMEM_LIMIT_BYTES = 56 * 1024 * 1024


def _params(n_axes):
    return pltpu.CompilerParams(dimension_semantics=("arbitrary",) * n_axes,
                                vmem_limit_bytes=VMEM_LIMIT_BYTES)


def _dot(a, b):
    return jnp.dot(a, b, preferred_element_type=F32)


def _norm_mod(x, g, shift, scale):
    y = x * lax.rsqrt(jnp.mean(x * x, axis=-1, keepdims=True) + EPS)
    return (y * g) * (1.0 + scale) + shift


def _ada_kernel(cond_ref, w_ref, b_ref, o_ref):
    a = jax.nn.silu(cond_ref[...]).astype(BF16)
    o_ref[...] = _dot(a, w_ref[...].astype(BF16)) + b_ref[...]


def _ada_mods(cond, ada_w, ada_b):
    depth, _, n = ada_w.shape
    tn = 1024
    return pl.pallas_call(
        _ada_kernel,
        grid=(depth, n // tn),
        in_specs=[
            pl.BlockSpec((MOD_ROWS, D_MODEL), lambda l, j: (0, 0)),
            pl.BlockSpec((None, D_MODEL, tn), lambda l, j: (l, 0, j)),
            pl.BlockSpec((None, 1, tn), lambda l, j: (l, 0, j)),
        ],
        out_specs=pl.BlockSpec((None, MOD_ROWS, tn), lambda l, j: (l, 0, j)),
        out_shape=jax.ShapeDtypeStruct((depth, MOD_ROWS, n), F32),
        compiler_params=_params(2),
        name="ada_mods",
    )(cond, ada_w, ada_b.reshape(depth, 1, n))


def _mod_spec(layer, chunk, n_grid_axes):
    if n_grid_axes == 1:
        return pl.BlockSpec((None, MOD_ROWS, D_MODEL), lambda i: (layer, 0, chunk))
    return pl.BlockSpec((None, MOD_ROWS, D_MODEL), lambda i, j: (layer, 0, chunk))


def _norm_matmul_kernel(x_ref, g_ref, sh_ref, sc_ref, w_ref, o_ref, h_scr, *, tiles_per_mod, mod_row0):
    i = pl.program_id(0)

    @pl.when(pl.program_id(1) == 0)
    def _():
        row = mod_row0 + i // tiles_per_mod
        h = _norm_mod(x_ref[...], g_ref[...], sh_ref[pl.ds(row, 1), :], sc_ref[pl.ds(row, 1), :])
        h_scr[...] = h.astype(BF16)

    o_ref[...] = _dot(h_scr[...], w_ref[...]).astype(o_ref.dtype)


def _norm_matmul(x2d, g, mods, layer, w, *, col0, n_cols, rows_per_mod, mod_row0, tm, tn, name):
    m = x2d.shape[0]
    tiles_per_mod = max(rows_per_mod // tm, 1)
    kern = functools.partial(_norm_matmul_kernel, tiles_per_mod=tiles_per_mod, mod_row0=mod_row0)
    cb0 = col0 // tn
    return pl.pallas_call(
        kern,
        grid=(m // tm, n_cols // tn),
        in_specs=[
            pl.BlockSpec((tm, D_MODEL), lambda i, j: (i, 0)),
            pl.BlockSpec((None, 1, D_MODEL), lambda i, j: (layer, 0, 0)),
            _mod_spec(layer, 0, 2),
            _mod_spec(layer, 1, 2),
            pl.BlockSpec((D_MODEL, tn), lambda i, j: (0, cb0 + j)),
        ],
        out_specs=pl.BlockSpec((tm, tn), lambda i, j: (i, j)),
        out_shape=jax.ShapeDtypeStruct((m, n_cols), BF16),
        scratch_shapes=[pltpu.VMEM((tm, D_MODEL), BF16)],
        compiler_params=_params(2),
        name=name,
    )(x2d, g, mods, mods, w)


def _bias_kernel(rpb_ref, bias_ref, cap_ref):
    h = pl.program_id(0)
    n_dc = 2 * WIN_COLS - 1
    n_dr = 2 * WIN_ROWS - 1
    q = lax.broadcasted_iota(jnp.int32, (GRID_W, 2 * GRID_W), 0)
    lane = lax.broadcasted_iota(jnp.int32, (GRID_W, 2 * GRID_W), 1)
    second = lane >= GRID_W
    k = jnp.where(second, lane - GRID_W, lane)
    dc = jnp.clip(k - q, -(WIN_COLS - 1), WIN_COLS - 1) + (WIN_COLS - 1)
    pairs = []
    for m in range(n_dr - 1):
        acc = jnp.zeros((GRID_W, 2 * GRID_W), F32)
        for d in range(n_dc):
            val = jnp.where(second, rpb_ref[h, (m + 1) * n_dc + d], rpb_ref[h, m * n_dc + d])
            acc = jnp.where(dc == d, val, acc)
        pairs.append(acc)
    for o in range(WIN_ROWS):
        for jj in range(WIN_ROWS // 2):
            bias_ref[o, :, jj * 2 * GRID_W:(jj + 1) * 2 * GRID_W] = pairs[o + 2 * jj]
    col_start = jnp.clip(q - WIN_COLS // 2, 0, GRID_W - WIN_COLS)
    cap = jnp.where((k >= col_start) & (k < col_start + WIN_COLS), UNMASKED_CAP, NEG_INF).astype(F32)
    for jj in range(WIN_ROWS // 2):
        cap_ref[:, jj * 2 * GRID_W:(jj + 1) * 2 * GRID_W] = cap


def _bias_tables(rpb):
    h = rpb.shape[0]
    kw = WIN_ROWS * GRID_W
    return pl.pallas_call(
        _bias_kernel,
        grid=(h,),
        in_specs=[pl.BlockSpec(memory_space=pltpu.SMEM)],
        out_specs=[
            pl.BlockSpec((None, WIN_ROWS, GRID_W, kw), lambda i: (i, 0, 0, 0)),
            pl.BlockSpec((GRID_W, kw), lambda i: (0, 0)),
        ],
        out_shape=[
            jax.ShapeDtypeStruct((h, WIN_ROWS, GRID_W, kw), F32),
            jax.ShapeDtypeStruct((GRID_W, kw), F32),
        ],
        compiler_params=_params(1),
        name="bias_tables",
    )(rpb.reshape(h, -1))


def _attn_kernel(q_ref, k_ref, v_ref, kc_ref, vc_ref, bias_ref, cap_ref, o_ref,
                 s_scr, e_scr, vx_scr, vcx_scr, acc_scr, *, rows):
    scale = HEAD_DIM ** -0.5
    kw = WIN_ROWS * GRID_W
    ctx_len = kc_ref.shape[0]
    nt = (((1,), (1,)), ((), ()))

    vx_scr[:, :HEAD_DIM] = v_ref[...]
    vx_scr[:, HEAD_DIM:] = jnp.ones((v_ref.shape[0], HEAD_DIM), BF16)
    vcx_scr[:, :HEAD_DIM] = vc_ref[...]
    vcx_scr[:, HEAD_DIM:] = jnp.ones((ctx_len, HEAD_DIM), BF16)

    s_scr[:, kw:] = lax.dot_general(q_ref[...], kc_ref[...], nt, preferred_element_type=F32) * scale

    def scores(r, carry):
        row_start = jnp.clip(r - WIN_ROWS // 2, 0, rows - WIN_ROWS)
        o = row_start - r + (WIN_ROWS - 1)
        q0 = pl.multiple_of(r * GRID_W, GRID_W)
        k0 = pl.multiple_of(row_start * GRID_W, GRID_W)
        s = lax.dot_general(q_ref[pl.ds(q0, GRID_W), :], k_ref[pl.ds(k0, kw), :], nt,
                            preferred_element_type=F32)
        s_scr[pl.ds(q0, GRID_W), :kw] = jnp.minimum(s * scale + bias_ref[o], cap_ref[...])
        return carry

    lax.fori_loop(0, rows, scores, 0, unroll=16)

    def expo(r, carry):
        q0 = pl.multiple_of(r * GRID_W, GRID_W)
        s = s_scr[pl.ds(q0, GRID_W), :]
        e_scr[pl.ds(q0, GRID_W), :] = jnp.exp(s - jnp.max(s, axis=-1, keepdims=True)).astype(BF16)
        return carry

    lax.fori_loop(0, rows, expo, 0, unroll=8)

    acc_scr[...] = _dot(e_scr[:, kw:], vcx_scr[...])

    def values(r, carry):
        row_start = jnp.clip(r - WIN_ROWS // 2, 0, rows - WIN_ROWS)
        q0 = pl.multiple_of(r * GRID_W, GRID_W)
        k0 = pl.multiple_of(row_start * GRID_W, GRID_W)
        t = acc_scr[pl.ds(q0, GRID_W), :] + _dot(e_scr[pl.ds(q0, GRID_W), :kw], vx_scr[pl.ds(k0, kw), :])
        o_ref[pl.ds(q0, GRID_W), :] = (t[:, :HEAD_DIM] / t[:, HEAD_DIM:]).astype(o_ref.dtype)
        return carry

    lax.fori_loop(0, rows, values, 0, unroll=16)


def _attention(qkv, ctx_kv, bias, cap, batch, seq, ctx_len):
    rows = seq // GRID_W
    kw = WIN_ROWS * GRID_W
    kern = functools.partial(_attn_kernel, rows=rows)
    return pl.pallas_call(
        kern,
        grid=(NA_HEADS, batch),
        in_specs=[
            pl.BlockSpec((seq, HEAD_DIM), lambda h, b: (b, h)),
            pl.BlockSpec((seq, HEAD_DIM), lambda h, b: (b, NA_HEADS + h)),
            pl.BlockSpec((seq, HEAD_DIM), lambda h, b: (b, 2 * NA_HEADS + h)),
            pl.BlockSpec((ctx_len, HEAD_DIM), lambda h, b: (b, h)),
            pl.BlockSpec((ctx_len, HEAD_DIM), lambda h, b: (b, NA_HEADS + h)),
            pl.BlockSpec((None, WIN_ROWS, GRID_W, kw), lambda h, b: (h, 0, 0, 0)),
            pl.BlockSpec((GRID_W, kw), lambda h, b: (0, 0)),
        ],
        out_specs=pl.BlockSpec((seq, HEAD_DIM), lambda h, b: (b, h)),
        out_shape=jax.ShapeDtypeStruct((batch * seq, NA_WIDTH), BF16),
        scratch_shapes=[
            pltpu.VMEM((seq, kw + ctx_len), F32),
            pltpu.VMEM((seq, kw + ctx_len), BF16),
            pltpu.VMEM((seq, 2 * HEAD_DIM), BF16),
            pltpu.VMEM((ctx_len, 2 * HEAD_DIM), BF16),
            pltpu.VMEM((seq, 2 * HEAD_DIM), F32),
        ],
        compiler_params=_params(2),
        name="nbr_attention",
    )(qkv, qkv, qkv, ctx_kv, ctx_kv, bias, cap)


def _residual_and_ffn_norm(i, tiles_per_mod, x_ref, mix, gate_ref, gn_ref, shf_ref, scf_ref, x1_ref, h_ref):
    row = i // tiles_per_mod
    x1 = x_ref[...] + gate_ref[pl.ds(row, 1), :] * mix
    x1_ref[...] = x1
    h = _norm_mod(x1, gn_ref[...], shf_ref[pl.ds(row, 1), :], scf_ref[pl.ds(row, 1), :])
    h_ref[...] = h.astype(BF16)


def _even_out_kernel(attn_ref, b_ref, c_ref, u_ref, cp_ref, up_ref, cn_ref, un_ref, cw_ref, wo_ref,
                     x_ref, gate_ref, gn_ref, shf_ref, scf_ref, x1_ref, h_ref, *, tiles_per_mod, halo):
    i = pl.program_id(0)
    tm = c_ref.shape[0]
    t = i % tiles_per_mod
    z = c_ref[...].astype(F32) * u_ref[...].astype(F32)
    z_before = cp_ref[halo - 1:halo, :].astype(F32) * up_ref[halo - 1:halo, :].astype(F32)
    z_after = cn_ref[0:1, :].astype(F32) * un_ref[0:1, :].astype(F32)
    z_before = jnp.where(t == 0, 0.0, z_before)
    z_after = jnp.where(t == tiles_per_mod - 1, 0.0, z_after)
    ridx = lax.broadcasted_iota(jnp.int32, (tm, 1), 0)
    z_prev = jnp.where(ridx == 0, z_before, pltpu.roll(z, 1, 0))
    z_next = jnp.where(ridx == tm - 1, z_after, pltpu.roll(z, tm - 1, 0))
    y = z_prev * cw_ref[0:1, :] + z * cw_ref[1:2, :] + z_next * cw_ref[2:3, :]
    conv = (b_ref[...].astype(F32) * y).astype(BF16)
    mix = _dot(attn_ref[...], wo_ref[0:NA_WIDTH, :]) + _dot(conv, wo_ref[NA_WIDTH:, :])
    _residual_and_ffn_norm(i, tiles_per_mod, x_ref, mix, gate_ref, gn_ref, shf_ref, scf_ref, x1_ref, h_ref)


def _even_out(attn, proj, conv_w, w_out, x2d, mods, norm_ffn_g, layer, seq, tm):
    m = x2d.shape[0]
    halo = 16
    hb = tm // halo
    n_halo_blocks = m // halo
    cb = NA_WIDTH * 3 // CONV_CH
    kern = functools.partial(_even_out_kernel, tiles_per_mod=seq // tm, halo=halo)
    main = lambda c: pl.BlockSpec((tm, CONV_CH), lambda i: (i, c))
    prev = lambda c: pl.BlockSpec((halo, CONV_CH), lambda i: (jnp.maximum(i * hb - 1, 0), c))
    nxt = lambda c: pl.BlockSpec((halo, CONV_CH), lambda i: (jnp.minimum((i + 1) * hb, n_halo_blocks - 1), c))
    return pl.pallas_call(
        kern,
        grid=(m // tm,),
        in_specs=[
            pl.BlockSpec((tm, NA_WIDTH), lambda i: (i, 0)),
            main(cb), main(cb + 1), main(cb + 2),
            prev(cb + 1), prev(cb + 2), nxt(cb + 1), nxt(cb + 2),
            pl.BlockSpec(conv_w.shape, lambda i: (0, 0)),
            pl.BlockSpec(w_out.shape, lambda i: (0, 0)),
            pl.BlockSpec((tm, D_MODEL), lambda i: (i, 0)),
            _mod_spec(layer, 2, 1),
            pl.BlockSpec((None, 1, D_MODEL), lambda i: (layer, 0, 0)),
            _mod_spec(layer, 3, 1),
            _mod_spec(layer, 4, 1),
        ],
        out_specs=[
            pl.BlockSpec((tm, D_MODEL), lambda i: (i, 0)),
            pl.BlockSpec((tm, D_MODEL), lambda i: (i, 0)),
        ],
        out_shape=[
            jax.ShapeDtypeStruct((m, D_MODEL), F32),
            jax.ShapeDtypeStruct((m, D_MODEL), BF16),
        ],
        compiler_params=_params(1),
        name="even_out",
    )(attn, proj, proj, proj, proj, proj, proj, proj, conv_w, w_out, x2d, mods, norm_ffn_g, mods, mods)


def _ffn_kernel(*refs, tiles_per_mod, modulated):
    if modulated:
        (h_ref, wg_ref, wu_ref, wd_ref, x_ref, gate_ref, gn_ref, sh_ref, sc_ref,
         x2_ref, y_ref, acc_ref) = refs
    else:
        h_ref, wg_ref, wu_ref, wd_ref, x_ref, gate_ref, gn_ref, y_ref, acc_ref = refs
    i = pl.program_id(0)
    f = pl.program_id(1)
    h = h_ref[...]
    t = (jax.nn.silu(_dot(h, wg_ref[...])) * _dot(h, wu_ref[...])).astype(BF16)
    part = _dot(t, wd_ref[...])

    @pl.when(f == 0)
    def _():
        acc_ref[...] = part

    @pl.when(f > 0)
    def _():
        acc_ref[...] += part

    @pl.when(f == pl.num_programs(1) - 1)
    def _():
        row = i // tiles_per_mod
        x2 = x_ref[...] + gate_ref[pl.ds(row, 1), :] * acc_ref[...]
        if modulated:
            x2_ref[...] = x2
            y = _norm_mod(x2, gn_ref[...], sh_ref[pl.ds(row, 1), :], sc_ref[pl.ds(row, 1), :])
        else:
            y = x2 * lax.rsqrt(jnp.mean(x2 * x2, axis=-1, keepdims=True) + EPS) * gn_ref[...]
        y_ref[...] = y.astype(y_ref.dtype)


def _ffn(h, w_gate, w_up, w_down, x2d, mods, layer, next_g, next_layer, seq, tm, tf):
    m = x2d.shape[0]
    hidden = w_gate.shape[1]
    modulated = next_layer is not None
    kern = functools.partial(_ffn_kernel, tiles_per_mod=seq // tm, modulated=modulated)
    row_spec = pl.BlockSpec((tm, D_MODEL), lambda i, f: (i, 0))
    in_specs = [
        row_spec,
        pl.BlockSpec((D_MODEL, tf), lambda i, f: (0, f)),
        pl.BlockSpec((D_MODEL, tf), lambda i, f: (0, f)),
        pl.BlockSpec((tf, D_MODEL), lambda i, f: (f, 0)),
        row_spec,
        _mod_spec(layer, 5, 2),
    ]
    args = [h, w_gate, w_up, w_down, x2d, mods]
    if modulated:
        in_specs += [pl.BlockSpec((None, 1, D_MODEL), lambda i, f: (next_layer, 0, 0)),
                     _mod_spec(next_layer, 0, 2), _mod_spec(next_layer, 1, 2)]
        args += [next_g, mods, mods]
        out_specs = [row_spec, row_spec]
        out_shape = [jax.ShapeDtypeStruct((m, D_MODEL), F32), jax.ShapeDtypeStruct((m, D_MODEL), BF16)]
    else:
        in_specs += [pl.BlockSpec((1, D_MODEL), lambda i, f: (0, 0))]
        args += [next_g]
        out_specs = row_spec
        out_shape = jax.ShapeDtypeStruct((m, D_MODEL), F32)
    return pl.pallas_call(
        kern,
        grid=(m // tm, hidden // tf),
        in_specs=in_specs,
        out_specs=out_specs,
        out_shape=out_shape,
        scratch_shapes=[pltpu.VMEM((tm, D_MODEL), F32)],
        compiler_params=_params(2),
        name="ffn_final" if not modulated else "ffn",
    )(*args)


def _odd_in_kernel(h_ref, w_ref, lg_ref, lb_ref, o_ref):
    z = jax.nn.gelu(_dot(h_ref[...], w_ref[...]), approximate=True)

    @pl.when(pl.program_id(1) == 0)
    def _():
        o_ref[...] = z.astype(o_ref.dtype)

    @pl.when(pl.program_id(1) == 1)
    def _():
        mu = jnp.mean(z, axis=-1, keepdims=True)
        zc = z - mu
        var = jnp.mean(zc * zc, axis=-1, keepdims=True)
        o_ref[...] = (zc * lax.rsqrt(var + EPS) * lg_ref[...] + lb_ref[...]).astype(o_ref.dtype)


def _odd_in(h, w_in, ln_g, ln_b, tm):
    m = h.shape[0]
    return pl.pallas_call(
        _odd_in_kernel,
        grid=(m // tm, 2),
        in_specs=[
            pl.BlockSpec((tm, D_MODEL), lambda i, j: (i, 0)),
            pl.BlockSpec((D_MODEL, D_MODEL), lambda i, j: (0, j)),
            pl.BlockSpec((1, D_MODEL), lambda i, j: (0, 0)),
            pl.BlockSpec((1, D_MODEL), lambda i, j: (0, 0)),
        ],
        out_specs=pl.BlockSpec((tm, D_MODEL), lambda i, j: (i, j)),
        out_shape=jax.ShapeDtypeStruct((m, 2 * D_MODEL), BF16),
        compiler_params=_params(2),
        name="odd_in",
    )(h, w_in, ln_g, ln_b)


def _odd_out_kernel(u_ref, v_ref, ws_ref, bs_ref, wo_ref, x_ref, gate_ref, gn_ref, shf_ref, scf_ref,
                    x1_ref, h_ref, t_scr, *, tiles_per_mod):
    i = pl.program_id(0)
    n_chunks = u_ref.shape[0] // CHUNK
    for g in range(SG_GROUPS):
        cols = slice(g * CHUNK, (g + 1) * CHUNK)
        vcat = jnp.concatenate([v_ref[n * CHUNK:(n + 1) * CHUNK, cols] for n in range(n_chunks)], axis=1)
        mixed = _dot(ws_ref[g], vcat)
        for n in range(n_chunks):
            rows = slice(n * CHUNK, (n + 1) * CHUNK)
            gated = u_ref[rows, cols].astype(F32) * (mixed[:, n * CHUNK:(n + 1) * CHUNK] + bs_ref[g])
            t_scr[rows, cols] = gated.astype(BF16)
    mix = _dot(t_scr[...], wo_ref[...])
    _residual_and_ffn_norm(i, tiles_per_mod, x_ref, mix, gate_ref, gn_ref, shf_ref, scf_ref, x1_ref, h_ref)


def _odd_out(uv, w_s, b_s, w_out, x2d, mods, norm_ffn_g, layer, seq, tm):
    m = x2d.shape[0]
    kern = functools.partial(_odd_out_kernel, tiles_per_mod=seq // tm)
    return pl.pallas_call(
        kern,
        grid=(m // tm,),
        in_specs=[
            pl.BlockSpec((tm, D_MODEL), lambda i: (i, 0)),
            pl.BlockSpec((tm, D_MODEL), lambda i: (i, 1)),
            pl.BlockSpec(w_s.shape, lambda i: (0, 0, 0)),
            pl.BlockSpec(b_s.shape, lambda i: (0, 0, 0)),
            pl.BlockSpec(w_out.shape, lambda i: (0, 0)),
            pl.BlockSpec((tm, D_MODEL), lambda i: (i, 0)),
            _mod_spec(layer, 2, 1),
            pl.BlockSpec((None, 1, D_MODEL), lambda i: (layer, 0, 0)),
            _mod_spec(layer, 3, 1),
            _mod_spec(layer, 4, 1),
        ],
        out_specs=[
            pl.BlockSpec((tm, D_MODEL), lambda i: (i, 0)),
            pl.BlockSpec((tm, D_MODEL), lambda i: (i, 0)),
        ],
        out_shape=[
            jax.ShapeDtypeStruct((m, D_MODEL), F32),
            jax.ShapeDtypeStruct((m, D_MODEL), BF16),
        ],
        scratch_shapes=[pltpu.VMEM((tm, D_MODEL), BF16)],
        compiler_params=_params(1),
        name="odd_out",
    )(uv, uv, w_s, b_s, w_out, x2d, mods, norm_ffn_g, mods, mods, )


def kernel(x, c, ctx, c_ctx, ada_w, ada_b, norm_mix_g, norm_ffn_g, ffn_w_gate, ffn_w_up, ffn_w_down,
           even_w_in, even_w_out, rpb, conv_w, odd_w_in, odd_w_out, sg_ln_g, sg_ln_b, sg_w, sg_b,
           final_norm_g):
    batch, seq, d = x.shape
    ctx_len = ctx.shape[1]
    depth = ada_w.shape[0]
    assert d == D_MODEL and depth == 2 and batch < MOD_ROWS and seq % (GRID_W * WIN_ROWS) == 0

    bf = lambda w: w.astype(BF16)
    x2d = x.reshape(batch * seq, d)
    ctx2d = ctx.reshape(batch * ctx_len, d)
    g_mix = norm_mix_g.reshape(depth, 1, d)
    g_ffn = norm_ffn_g.reshape(depth, 1, d)

    cond = jnp.concatenate([c, c_ctx[None], jnp.zeros((MOD_ROWS - batch - 1, d), F32)], axis=0)
    mods = _ada_mods(cond, ada_w, ada_b)

    w_in0 = bf(even_w_in[0])
    proj = _norm_matmul(x2d, g_mix, mods, 0, w_in0, col0=0, n_cols=w_in0.shape[1], rows_per_mod=seq,
                        mod_row0=0, tm=1024, tn=1024, name="even_in")
    ctx_kv = _norm_matmul(ctx2d, g_mix, mods, 0, w_in0, col0=NA_WIDTH, n_cols=2 * NA_WIDTH,
                          rows_per_mod=batch * ctx_len, mod_row0=batch, tm=1024, tn=1024, name="ctx_kv")
    bias, cap = _bias_tables(rpb[0])
    attn = _attention(proj, ctx_kv, bias, cap, batch, seq, ctx_len)
    x1, h = _even_out(attn, proj, conv_w[0], bf(even_w_out[0]), x2d, mods, g_ffn, 0, seq, tm=512)
    x2, h = _ffn(h, bf(ffn_w_gate[0]), bf(ffn_w_up[0]), bf(ffn_w_down[0]), x1, mods, 0,
                 g_mix, 1, seq, tm=512, tf=512)

    uv = _odd_in(h, bf(odd_w_in[0]), sg_ln_g[0][None], sg_ln_b[0][None], tm=512)
    b_s = jnp.broadcast_to(sg_b[0][:, :, None], (SG_GROUPS, CHUNK, CHUNK))
    x1, h = _odd_out(uv, bf(sg_w[0]), b_s, bf(odd_w_out[0]), x2, mods, g_ffn, 1, seq, tm=512)
    out = _ffn(h, bf(ffn_w_gate[1]), bf(ffn_w_up[1]), bf(ffn_w_down[1]), x1, mods, 1,
               final_norm_g[None], None, seq, tm=512, tf=512)
    return out.reshape(batch, seq, d)
```

```python
import functools

import jax
import jax.numpy as jnp
from jax import lax
from jax.experimental import pallas as pl
from jax.experimental.pallas import tpu as pltpu

D_MODEL = 2048
GRID_W = 64
NA_HEADS = 8
HEAD_DIM = 128
NA_WIDTH = NA_HEADS * HEAD_DIM
CONV_CH = D_MODEL - NA_WIDTH
WIN_ROWS = 8
WIN_COLS = 16
CHUNK = 128
SG_GROUPS = 16
EPS = 1e-6
NEG_INF = -1e30
UNMASKED_CAP = 3e38
MOD_ROWS = 16
FFN_ROW_CHUNKS = 2
ROW_CHUNK = 256

F32 = jnp.float32
BF16 = jnp.bfloat16

VMEM_LIMIT_BYTES = 56 * 1024 * 1024


def _params(n_axes):
    return pltpu.CompilerParams(dimension_semantics=("arbitrary",) * n_axes,
                                vmem_limit_bytes=VMEM_LIMIT_BYTES)


def _resident_spec(shape):
    zeros = (0,) * len(shape)
    return pl.BlockSpec(shape, lambda *_: zeros, pipeline_mode=pl.Buffered(1))


def _dot(a, b):
    return jnp.dot(a, b, preferred_element_type=F32)


def _norm_mod(x, g, shift, scale):
    y = x * lax.rsqrt(jnp.mean(x * x, axis=-1, keepdims=True) + EPS)
    return (y * g) * (1.0 + scale) + shift


def _ada_kernel(cond_ref, w_ref, b_ref, o_ref):
    a = jax.nn.silu(cond_ref[...]).astype(BF16)
    o_ref[...] = _dot(a, w_ref[...].astype(BF16)) + b_ref[...]


def _ada_mods(cond, ada_w, ada_b):
    depth, _, n = ada_w.shape
    tn = 1024
    return pl.pallas_call(
        _ada_kernel,
        grid=(depth, n // tn),
        in_specs=[
            pl.BlockSpec((MOD_ROWS, D_MODEL), lambda l, j: (0, 0)),
            pl.BlockSpec((None, D_MODEL, tn), lambda l, j: (l, 0, j)),
            pl.BlockSpec((None, 1, tn), lambda l, j: (l, 0, j)),
        ],
        out_specs=pl.BlockSpec((None, MOD_ROWS, tn), lambda l, j: (l, 0, j)),
        out_shape=jax.ShapeDtypeStruct((depth, MOD_ROWS, n), F32),
        compiler_params=_params(2),
        name="ada_mods",
    )(cond, ada_w, ada_b.reshape(depth, 1, n))


def _mod_spec(layer, chunk, n_grid_axes):
    if n_grid_axes == 1:
        return pl.BlockSpec((None, MOD_ROWS, D_MODEL), lambda i: (layer, 0, chunk))
    return pl.BlockSpec((None, MOD_ROWS, D_MODEL), lambda i, j: (layer, 0, chunk))


def _norm_matmul_kernel(x_ref, g_ref, sh_ref, sc_ref, w_ref, o_ref, h_scr, *, tiles_per_mod, mod_row0):
    i = pl.program_id(0)

    @pl.when(pl.program_id(1) == 0)
    def _():
        row = mod_row0 + i // tiles_per_mod
        h = _norm_mod(x_ref[...], g_ref[...], sh_ref[pl.ds(row, 1), :], sc_ref[pl.ds(row, 1), :])
        h_scr[...] = h.astype(BF16)

    o_ref[...] = _dot(h_scr[...], w_ref[...]).astype(o_ref.dtype)


def _norm_matmul(x2d, g, mods, layer, w, *, col0, n_cols, rows_per_mod, mod_row0, tm, tn, name):
    m = x2d.shape[0]
    tiles_per_mod = max(rows_per_mod // tm, 1)
    kern = functools.partial(_norm_matmul_kernel, tiles_per_mod=tiles_per_mod, mod_row0=mod_row0)
    cb0 = col0 // tn
    return pl.pallas_call(
        kern,
        grid=(m // tm, n_cols // tn),
        in_specs=[
            pl.BlockSpec((tm, D_MODEL), lambda i, j: (i, 0)),
            pl.BlockSpec((None, 1, D_MODEL), lambda i, j: (layer, 0, 0)),
            _mod_spec(layer, 0, 2),
            _mod_spec(layer, 1, 2),
            pl.BlockSpec((D_MODEL, tn), lambda i, j: (0, cb0 + j)),
        ],
        out_specs=pl.BlockSpec((tm, tn), lambda i, j: (i, j)),
        out_shape=jax.ShapeDtypeStruct((m, n_cols), BF16),
        scratch_shapes=[pltpu.VMEM((tm, D_MODEL), BF16)],
        compiler_params=_params(2),
        name=name,
    )(x2d, g, mods, mods, w)


def _bias_kernel(rpb_ref, bias_ref, cap_ref):
    h = pl.program_id(0)
    n_dc = 2 * WIN_COLS - 1
    n_dr = 2 * WIN_ROWS - 1
    q = lax.broadcasted_iota(jnp.int32, (GRID_W, 2 * GRID_W), 0)
    lane = lax.broadcasted_iota(jnp.int32, (GRID_W, 2 * GRID_W), 1)
    second = lane >= GRID_W
    k = jnp.where(second, lane - GRID_W, lane)
    dc = jnp.clip(k - q, -(WIN_COLS - 1), WIN_COLS - 1) + (WIN_COLS - 1)
    pairs = []
    for m in range(n_dr - 1):
        acc = jnp.zeros((GRID_W, 2 * GRID_W), F32)
        for d in range(n_dc):
            val = jnp.where(second, rpb_ref[h, (m + 1) * n_dc + d], rpb_ref[h, m * n_dc + d])
            acc = jnp.where(dc == d, val, acc)
        pairs.append(acc)
    for o in range(WIN_ROWS):
        for jj in range(WIN_ROWS // 2):
            bias_ref[o, :, jj * 2 * GRID_W:(jj + 1) * 2 * GRID_W] = pairs[o + 2 * jj]
    col_start = jnp.clip(q - WIN_COLS // 2, 0, GRID_W - WIN_COLS)
    cap = jnp.where((k >= col_start) & (k < col_start + WIN_COLS), UNMASKED_CAP, NEG_INF).astype(F32)
    for jj in range(WIN_ROWS // 2):
        cap_ref[:, jj * 2 * GRID_W:(jj + 1) * 2 * GRID_W] = cap


def _bias_tables(rpb):
    h = rpb.shape[0]
    kw = WIN_ROWS * GRID_W
    return pl.pallas_call(
        _bias_kernel,
        grid=(h,),
        in_specs=[pl.BlockSpec(memory_space=pltpu.SMEM)],
        out_specs=[
            pl.BlockSpec((None, WIN_ROWS, GRID_W, kw), lambda i: (i, 0, 0, 0)),
            pl.BlockSpec((GRID_W, kw), lambda i: (0, 0)),
        ],
        out_shape=[
            jax.ShapeDtypeStruct((h, WIN_ROWS, GRID_W, kw), F32),
            jax.ShapeDtypeStruct((GRID_W, kw), F32),
        ],
        compiler_params=_params(1),
        name="bias_tables",
    )(rpb.reshape(h, -1))


def _attn_kernel(q_ref, k_ref, v_ref, kc_ref, vc_ref, bias_ref, cap_ref, o_ref,
                 s_scr, e_scr, vx_scr, vcx_scr, acc_scr, *, rows):
    scale = HEAD_DIM ** -0.5
    kw = WIN_ROWS * GRID_W
    ctx_len = kc_ref.shape[0]
    nt = (((1,), (1,)), ((), ()))

    vx_scr[:, :HEAD_DIM] = v_ref[...]
    vx_scr[:, HEAD_DIM:] = jnp.ones((v_ref.shape[0], HEAD_DIM), BF16)
    vcx_scr[:, :HEAD_DIM] = vc_ref[...]
    vcx_scr[:, HEAD_DIM:] = jnp.ones((ctx_len, HEAD_DIM), BF16)

    s_scr[:, kw:] = lax.dot_general(q_ref[...], kc_ref[...], nt, preferred_element_type=F32) * scale

    def scores(r, carry):
        row_start = jnp.clip(r - WIN_ROWS // 2, 0, rows - WIN_ROWS)
        o = row_start - r + (WIN_ROWS - 1)
        q0 = pl.multiple_of(r * GRID_W, GRID_W)
        k0 = pl.multiple_of(row_start * GRID_W, GRID_W)
        s = lax.dot_general(q_ref[pl.ds(q0, GRID_W), :], k_ref[pl.ds(k0, kw), :], nt,
                            preferred_element_type=F32)
        s_scr[pl.ds(q0, GRID_W), :kw] = jnp.minimum(s * scale + bias_ref[o], cap_ref[...])
        return carry

    lax.fori_loop(0, rows, scores, 0, unroll=16)

    def expo(r, carry):
        q0 = pl.multiple_of(r * GRID_W, GRID_W)
        s = s_scr[pl.ds(q0, GRID_W), :]
        e_scr[pl.ds(q0, GRID_W), :] = jnp.exp(s - jnp.max(s, axis=-1, keepdims=True)).astype(BF16)
        return carry

    lax.fori_loop(0, rows, expo, 0, unroll=8)

    acc_scr[...] = _dot(e_scr[:, kw:], vcx_scr[...])

    def values(r, carry):
        row_start = jnp.clip(r - WIN_ROWS // 2, 0, rows - WIN_ROWS)
        q0 = pl.multiple_of(r * GRID_W, GRID_W)
        k0 = pl.multiple_of(row_start * GRID_W, GRID_W)
        t = acc_scr[pl.ds(q0, GRID_W), :] + _dot(e_scr[pl.ds(q0, GRID_W), :kw], vx_scr[pl.ds(k0, kw), :])
        o_ref[pl.ds(q0, GRID_W), :] = (t[:, :HEAD_DIM] / t[:, HEAD_DIM:]).astype(o_ref.dtype)
        return carry

    lax.fori_loop(0, rows, values, 0, unroll=16)


def _attention(qkv, ctx_kv, bias, cap, batch, seq, ctx_len):
    rows = seq // GRID_W
    kw = WIN_ROWS * GRID_W
    kern = functools.partial(_attn_kernel, rows=rows)
    return pl.pallas_call(
        kern,
        grid=(NA_HEADS, batch),
        in_specs=[
            pl.BlockSpec((seq, HEAD_DIM), lambda h, b: (b, h)),
            pl.BlockSpec((seq, HEAD_DIM), lambda h, b: (b, NA_HEADS + h)),
            pl.BlockSpec((seq, HEAD_DIM), lambda h, b: (b, 2 * NA_HEADS + h)),
            pl.BlockSpec((ctx_len, HEAD_DIM), lambda h, b: (b, h)),
            pl.BlockSpec((ctx_len, HEAD_DIM), lambda h, b: (b, NA_HEADS + h)),
            pl.BlockSpec((None, WIN_ROWS, GRID_W, kw), lambda h, b: (h, 0, 0, 0)),
            pl.BlockSpec((GRID_W, kw), lambda h, b: (0, 0)),
        ],
        out_specs=pl.BlockSpec((seq, HEAD_DIM), lambda h, b: (b, h)),
        out_shape=jax.ShapeDtypeStruct((batch * seq, NA_WIDTH), BF16),
        scratch_shapes=[
            pltpu.VMEM((seq, kw + ctx_len), F32),
            pltpu.VMEM((seq, kw + ctx_len), BF16),
            pltpu.VMEM((seq, 2 * HEAD_DIM), BF16),
            pltpu.VMEM((ctx_len, 2 * HEAD_DIM), BF16),
            pltpu.VMEM((seq, 2 * HEAD_DIM), F32),
        ],
        compiler_params=_params(2),
        name="nbr_attention",
    )(qkv, qkv, qkv, ctx_kv, ctx_kv, bias, cap)


def _residual_and_ffn_norm(rows, mod_row, x_ref, mix, gate_ref, gn_ref, shf_ref, scf_ref, x1_ref, h_ref):
    x1 = x_ref[rows, :] + gate_ref[pl.ds(mod_row, 1), :] * mix
    x1_ref[rows, :] = x1
    h = _norm_mod(x1, gn_ref[...], shf_ref[pl.ds(mod_row, 1), :], scf_ref[pl.ds(mod_row, 1), :])
    h_ref[rows, :] = h.astype(BF16)


def _row_chunks(tm):
    return [slice(r, r + ROW_CHUNK) for r in range(0, tm, ROW_CHUNK)]


def _skewed(chunks, matmul_fn, epilogue_fn):
    pending = None
    for rows in chunks:
        acc = matmul_fn(rows)
        if pending is not None:
            epilogue_fn(*pending)
        pending = (rows, acc)
    epilogue_fn(*pending)


def _even_out_kernel(attn_ref, b_ref, c_ref, u_ref, cp_ref, up_ref, cn_ref, un_ref, cw_ref, wo_ref,
                     x_ref, gate_ref, gn_ref, shf_ref, scf_ref, x1_ref, h_ref, conv_scr, *, tiles_per_mod, halo):
    i = pl.program_id(0)
    tm = c_ref.shape[0]
    t = i % tiles_per_mod
    chunks = _row_chunks(tm)
    mix_attn = [_dot(attn_ref[rows, :], wo_ref[0:NA_WIDTH, :]) for rows in chunks]
    z = c_ref[...].astype(F32) * u_ref[...].astype(F32)
    z_before = cp_ref[halo - 1:halo, :].astype(F32) * up_ref[halo - 1:halo, :].astype(F32)
    z_after = cn_ref[0:1, :].astype(F32) * un_ref[0:1, :].astype(F32)
    z_before = jnp.where(t == 0, 0.0, z_before)
    z_after = jnp.where(t == tiles_per_mod - 1, 0.0, z_after)
    ridx = lax.broadcasted_iota(jnp.int32, (tm, 1), 0)
    z_prev = jnp.where(ridx == 0, z_before, pltpu.roll(z, 1, 0))
    z_next = jnp.where(ridx == tm - 1, z_after, pltpu.roll(z, tm - 1, 0))
    y = z_prev * cw_ref[0:1, :] + z * cw_ref[1:2, :] + z_next * cw_ref[2:3, :]
    conv_scr[...] = (b_ref[...].astype(F32) * y).astype(BF16)
    _skewed(
        chunks,
        lambda rows: mix_attn[chunks.index(rows)] + _dot(conv_scr[rows, :], wo_ref[NA_WIDTH:, :]),
        lambda rows, mix: _residual_and_ffn_norm(rows, i // tiles_per_mod, x_ref, mix, gate_ref, gn_ref,
                                                 shf_ref, scf_ref, x1_ref, h_ref))


def _even_out(attn, proj, conv_w, w_out, x2d, mods, norm_ffn_g, layer, seq, tm):
    m = x2d.shape[0]
    halo = 16
    hb = tm // halo
    n_halo_blocks = m // halo
    cb = NA_WIDTH * 3 // CONV_CH
    kern = functools.partial(_even_out_kernel, tiles_per_mod=seq // tm, halo=halo)
    main = lambda c: pl.BlockSpec((tm, CONV_CH), lambda i: (i, c))
    prev = lambda c: pl.BlockSpec((halo, CONV_CH), lambda i: (jnp.maximum(i * hb - 1, 0), c))
    nxt = lambda c: pl.BlockSpec((halo, CONV_CH), lambda i: (jnp.minimum((i + 1) * hb, n_halo_blocks - 1), c))
    return pl.pallas_call(
        kern,
        grid=(m // tm,),
        in_specs=[
            pl.BlockSpec((tm, NA_WIDTH), lambda i: (i, 0)),
            main(cb), main(cb + 1), main(cb + 2),
            prev(cb + 1), prev(cb + 2), nxt(cb + 1), nxt(cb + 2),
            _resident_spec(conv_w.shape),
            _resident_spec(w_out.shape),
            pl.BlockSpec((tm, D_MODEL), lambda i: (i, 0)),
            _mod_spec(layer, 2, 1),
            pl.BlockSpec((None, 1, D_MODEL), lambda i: (layer, 0, 0)),
            _mod_spec(layer, 3, 1),
            _mod_spec(layer, 4, 1),
        ],
        out_specs=[
            pl.BlockSpec((tm, D_MODEL), lambda i: (i, 0)),
            pl.BlockSpec((tm, D_MODEL), lambda i: (i, 0)),
        ],
        out_shape=[
            jax.ShapeDtypeStruct((m, D_MODEL), F32),
            jax.ShapeDtypeStruct((m, D_MODEL), BF16),
        ],
        scratch_shapes=[pltpu.VMEM((tm, CONV_CH), BF16)],
        compiler_params=_params(1),
        name="even_out",
    )(attn, proj, proj, proj, proj, proj, proj, proj, conv_w, w_out, x2d, mods, norm_ffn_g, mods, mods)


def _ffn_kernel(*refs, tiles_per_mod, modulated):
    if modulated:
        (h_ref, wg_ref, wu_ref, wd_ref, x_ref, gate_ref, gn_ref, sh_ref, sc_ref,
         x2_ref, y_ref, acc_ref) = refs
    else:
        h_ref, wg_ref, wu_ref, wd_ref, x_ref, gate_ref, gn_ref, y_ref, acc_ref = refs
    i = pl.program_id(0)
    f = pl.program_id(1)

    @pl.when(f == 0)
    def _():
        acc_ref[...] = jnp.zeros_like(acc_ref)

    tm = h_ref.shape[0]
    rc = tm // FFN_ROW_CHUNKS
    ts = []
    for c in range(FFN_ROW_CHUNKS):
        h = h_ref[c * rc:(c + 1) * rc, :]
        ts.append((jax.nn.silu(_dot(h, wg_ref[...])) * _dot(h, wu_ref[...])).astype(BF16))
    for c in range(FFN_ROW_CHUNKS):
        acc_ref[c * rc:(c + 1) * rc, :] += _dot(ts[c], wd_ref[...])

    @pl.when(f == pl.num_programs(1) - 1)
    def _():
        row = i // tiles_per_mod
        x2 = x_ref[...] + gate_ref[pl.ds(row, 1), :] * acc_ref[...]
        if modulated:
            x2_ref[...] = x2
            y = _norm_mod(x2, gn_ref[...], sh_ref[pl.ds(row, 1), :], sc_ref[pl.ds(row, 1), :])
        else:
            y = x2 * lax.rsqrt(jnp.mean(x2 * x2, axis=-1, keepdims=True) + EPS) * gn_ref[...]
        y_ref[...] = y.astype(y_ref.dtype)


def _ffn(h, w_gate, w_up, w_down, x2d, mods, layer, next_g, next_layer, seq, tm, tf):
    m = x2d.shape[0]
    hidden = w_gate.shape[1]
    modulated = next_layer is not None
    kern = functools.partial(_ffn_kernel, tiles_per_mod=seq // tm, modulated=modulated)
    row_spec = pl.BlockSpec((tm, D_MODEL), lambda i, f: (i, 0))
    in_specs = [
        row_spec,
        pl.BlockSpec((D_MODEL, tf), lambda i, f: (0, f)),
        pl.BlockSpec((D_MODEL, tf), lambda i, f: (0, f)),
        pl.BlockSpec((tf, D_MODEL), lambda i, f: (f, 0)),
        row_spec,
        _mod_spec(layer, 5, 2),
    ]
    args = [h, w_gate, w_up, w_down, x2d, mods]
    if modulated:
        in_specs += [pl.BlockSpec((None, 1, D_MODEL), lambda i, f: (next_layer, 0, 0)),
                     _mod_spec(next_layer, 0, 2), _mod_spec(next_layer, 1, 2)]
        args += [next_g, mods, mods]
        out_specs = [row_spec, row_spec]
        out_shape = [jax.ShapeDtypeStruct((m, D_MODEL), F32), jax.ShapeDtypeStruct((m, D_MODEL), BF16)]
    else:
        in_specs += [pl.BlockSpec((1, D_MODEL), lambda i, f: (0, 0))]
        args += [next_g]
        out_specs = row_spec
        out_shape = jax.ShapeDtypeStruct((m, D_MODEL), F32)
    return pl.pallas_call(
        kern,
        grid=(m // tm, hidden // tf),
        in_specs=in_specs,
        out_specs=out_specs,
        out_shape=out_shape,
        scratch_shapes=[pltpu.VMEM((tm, D_MODEL), F32)],
        compiler_params=_params(2),
        name="ffn_final" if not modulated else "ffn",
    )(*args)


def _odd_in_kernel(h_ref, w_ref, lg_ref, lb_ref, u_ref, v_ref):
    def matmuls(rows):
        h = h_ref[rows, :]
        return _dot(h, w_ref[:, :D_MODEL]), _dot(h, w_ref[:, D_MODEL:])

    def epilogue(rows, acc):
        u_ref[rows, :] = jax.nn.gelu(acc[0], approximate=True).astype(BF16)
        z = jax.nn.gelu(acc[1], approximate=True)
        zc = z - jnp.mean(z, axis=-1, keepdims=True)
        var = jnp.mean(zc * zc, axis=-1, keepdims=True)
        v_ref[rows, :] = (zc * lax.rsqrt(var + EPS) * lg_ref[...] + lb_ref[...]).astype(BF16)

    _skewed(_row_chunks(h_ref.shape[0]), matmuls, epilogue)


def _odd_in(h, w_in, ln_g, ln_b, tm):
    m = h.shape[0]
    row_spec = pl.BlockSpec((tm, D_MODEL), lambda i: (i, 0))
    return pl.pallas_call(
        _odd_in_kernel,
        grid=(m // tm,),
        in_specs=[
            row_spec,
            _resident_spec(w_in.shape),
            pl.BlockSpec((1, D_MODEL), lambda i: (0, 0)),
            pl.BlockSpec((1, D_MODEL), lambda i: (0, 0)),
        ],
        out_specs=[row_spec, row_spec],
        out_shape=[jax.ShapeDtypeStruct((m, D_MODEL), BF16)] * 2,
        compiler_params=_params(1),
        name="odd_in",
    )(h, w_in, ln_g, ln_b)


def _odd_out_kernel(u_ref, v_ref, ws_ref, bs_ref, wo_ref, x_ref, gate_ref, gn_ref, shf_ref, scf_ref,
                    x1_ref, h_ref, t_scr, *, tiles_per_mod):
    i = pl.program_id(0)
    n_chunks = u_ref.shape[0] // CHUNK
    for g in range(SG_GROUPS):
        cols = slice(g * CHUNK, (g + 1) * CHUNK)
        vcat = jnp.concatenate([v_ref[n * CHUNK:(n + 1) * CHUNK, cols] for n in range(n_chunks)], axis=1)
        mixed = _dot(ws_ref[g], vcat)
        for n in range(n_chunks):
            rows = slice(n * CHUNK, (n + 1) * CHUNK)
            gated = u_ref[rows, cols].astype(F32) * (mixed[:, n * CHUNK:(n + 1) * CHUNK] + bs_ref[g])
            t_scr[rows, cols] = gated.astype(BF16)
    _skewed(
        _row_chunks(u_ref.shape[0]),
        lambda rows: _dot(t_scr[rows, :], wo_ref[...]),
        lambda rows, mix: _residual_and_ffn_norm(rows, i // tiles_per_mod, x_ref, mix, gate_ref, gn_ref,
                                                 shf_ref, scf_ref, x1_ref, h_ref))


def _odd_out(u, v, w_s, b_s, w_out, x2d, mods, norm_ffn_g, layer, seq, tm):
    m = x2d.shape[0]
    kern = functools.partial(_odd_out_kernel, tiles_per_mod=seq // tm)
    return pl.pallas_call(
        kern,
        grid=(m // tm,),
        in_specs=[
            pl.BlockSpec((tm, D_MODEL), lambda i: (i, 0)),
            pl.BlockSpec((tm, D_MODEL), lambda i: (i, 0)),
            _resident_spec(w_s.shape),
            _resident_spec(b_s.shape),
            _resident_spec(w_out.shape),
            pl.BlockSpec((tm, D_MODEL), lambda i: (i, 0)),
            _mod_spec(layer, 2, 1),
            pl.BlockSpec((None, 1, D_MODEL), lambda i: (layer, 0, 0)),
            _mod_spec(layer, 3, 1),
            _mod_spec(layer, 4, 1),
        ],
        out_specs=[
            pl.BlockSpec((tm, D_MODEL), lambda i: (i, 0)),
            pl.BlockSpec((tm, D_MODEL), lambda i: (i, 0)),
        ],
        out_shape=[
            jax.ShapeDtypeStruct((m, D_MODEL), F32),
            jax.ShapeDtypeStruct((m, D_MODEL), BF16),
        ],
        scratch_shapes=[pltpu.VMEM((tm, D_MODEL), BF16)],
        compiler_params=_params(1),
        name="odd_out",
    )(u, v, w_s, b_s, w_out, x2d, mods, norm_ffn_g, mods, mods)


def kernel(x, c, ctx, c_ctx, ada_w, ada_b, norm_mix_g, norm_ffn_g, ffn_w_gate, ffn_w_up, ffn_w_down,
           even_w_in, even_w_out, rpb, conv_w, odd_w_in, odd_w_out, sg_ln_g, sg_ln_b, sg_w, sg_b,
           final_norm_g):
    batch, seq, d = x.shape
    ctx_len = ctx.shape[1]
    depth = ada_w.shape[0]
    assert d == D_MODEL and depth == 2 and batch < MOD_ROWS and seq % (GRID_W * WIN_ROWS) == 0

    bf = lambda w: w.astype(BF16)
    x2d = x.reshape(batch * seq, d)
    ctx2d = ctx.reshape(batch * ctx_len, d)
    g_mix = norm_mix_g.reshape(depth, 1, d)
    g_ffn = norm_ffn_g.reshape(depth, 1, d)

    cond = jnp.concatenate([c, c_ctx[None], jnp.zeros((MOD_ROWS - batch - 1, d), F32)], axis=0)
    mods = _ada_mods(cond, ada_w, ada_b)

    w_in0 = bf(even_w_in[0])
    proj = _norm_matmul(x2d, g_mix, mods, 0, w_in0, col0=0, n_cols=w_in0.shape[1], rows_per_mod=seq,
                        mod_row0=0, tm=1024, tn=1024, name="even_in")
    ctx_kv = _norm_matmul(ctx2d, g_mix, mods, 0, w_in0, col0=NA_WIDTH, n_cols=2 * NA_WIDTH,
                          rows_per_mod=batch * ctx_len, mod_row0=batch, tm=1024, tn=1024, name="ctx_kv")
    bias, cap = _bias_tables(rpb[0])
    attn = _attention(proj, ctx_kv, bias, cap, batch, seq, ctx_len)
    x1, h = _even_out(attn, proj, conv_w[0], bf(even_w_out[0]), x2d, mods, g_ffn, 0, seq, tm=512)
    x2, h = _ffn(h, bf(ffn_w_gate[0]), bf(ffn_w_up[0]), bf(ffn_w_down[0]), x1, mods, 0,
                 g_mix, 1, seq, tm=512, tf=512)

    u, v = _odd_in(h, bf(odd_w_in[0]), sg_ln_g[0][None], sg_ln_b[0][None], tm=512)
    b_s = jnp.broadcast_to(sg_b[0][:, :, None], (SG_GROUPS, CHUNK, CHUNK))
    x1, h = _odd_out(u, v, bf(sg_w[0]), b_s, bf(odd_w_out[0]), x2, mods, g_ffn, 1, seq, tm=512)
    out = _ffn(h, bf(ffn_w_gate[1]), bf(ffn_w_up[1]), bf(ffn_w_down[1]), x1, mods, 1,
               final_norm_g[None], None, seq, tm=512, tf=512)
    return out.reshape(batch, seq, d)
```

```python
import functools

import jax
import jax.numpy as jnp
from jax import lax
from jax.experimental import pallas as pl
from jax.experimental.pallas import tpu as pltpu

D_MODEL = 2048
GRID_W = 64
NA_HEADS = 8
HEAD_DIM = 128
NA_WIDTH = NA_HEADS * HEAD_DIM
CONV_CH = D_MODEL - NA_WIDTH
WIN_ROWS = 8
WIN_COLS = 16
CHUNK = 128
SG_GROUPS = 16
EPS = 1e-6
NEG_INF = -1e30
UNMASKED_CAP = 3e38
MOD_ROWS = 16
FFN_TILE = 512
FFN_ROW_CHUNKS = 2
ROW_CHUNK = 256

F32 = jnp.float32
BF16 = jnp.bfloat16
BF16_SUBLANES = 16

VMEM_LIMIT_BYTES = 56 * 1024 * 1024


def _params(n_axes):
    return pltpu.CompilerParams(dimension_semantics=("arbitrary",) * n_axes,
                                vmem_limit_bytes=VMEM_LIMIT_BYTES)


def _resident_spec(shape):
    zeros = (0,) * len(shape)
    return pl.BlockSpec(shape, lambda *_: zeros, pipeline_mode=pl.Buffered(1))


def _dot(a, b):
    return jnp.dot(a, b, preferred_element_type=F32)


def _norm_mod(x, g, shift, scale):
    y = x * lax.rsqrt(jnp.mean(x * x, axis=-1, keepdims=True) + EPS)
    return (y * g) * (1.0 + scale) + shift


def _linear_step(ids, grid):
    step = ids[0]
    for axis in range(1, len(grid)):
        step = step * grid[axis] + ids[axis]
    return step


class _Cast:
    def __init__(self, stacked_w, layer, grid, col_tile=None):
        _, rows, cols = stacked_w.shape
        n_steps = 1
        for g in grid:
            n_steps *= g
        self.w = stacked_w
        self.grid = grid
        self.col_tile = col_tile
        self.n_blocks = nb = max(n for n in range(1, n_steps + 1)
                                 if rows % n == 0 and (rows // n) % BF16_SUBLANES == 0)
        rb = rows // nb
        block = lambda *ids: jnp.minimum(_linear_step(ids, grid), nb - 1)
        self.in_spec = pl.BlockSpec((None, rb, cols), lambda *ids: (layer, block(*ids), 0))
        if col_tile is None:
            self.out_shape = jax.ShapeDtypeStruct((rows, cols), BF16)
            self.out_spec = pl.BlockSpec((rb, cols), lambda *ids: (block(*ids), 0))
        else:
            nt = cols // col_tile
            self.out_shape = jax.ShapeDtypeStruct((nt, rows, col_tile), BF16)
            self.out_spec = pl.BlockSpec((nt, rb, col_tile), lambda *ids: (0, block(*ids), 0))

    def run(self, w_ref, o_ref):
        step = _linear_step([pl.program_id(a) for a in range(len(self.grid))], self.grid)

        @pl.when(step < self.n_blocks)
        def _():
            if self.col_tile is None:
                o_ref[...] = w_ref[...].astype(BF16)
            else:
                for t in range(o_ref.shape[0]):
                    o_ref[t] = w_ref[:, t * self.col_tile:(t + 1) * self.col_tile].astype(BF16)


def _with_casts(body, n_in, n_out, casts):
    n_c = len(casts)

    def kern(*refs):
        ins, rest = refs[:n_in], refs[n_in:]
        cast_ins, rest = rest[:n_c], rest[n_c:]
        outs, rest = rest[:n_out], rest[n_out:]
        cast_outs, scratch = rest[:n_c], rest[n_c:]
        body(*ins, *outs, *scratch)
        for cast, w_ref, o_ref in zip(casts, cast_ins, cast_outs):
            cast.run(w_ref, o_ref)

    return kern


def _ada_kernel(cond_ref, w_ref, b_ref, o_ref):
    a = jax.nn.silu(cond_ref[...]).astype(BF16)
    o_ref[...] = _dot(a, w_ref[...].astype(BF16)) + b_ref[...]


def _ada_mods(cond, ada_w, ada_b):
    depth, _, n = ada_w.shape
    tn = 1024
    return pl.pallas_call(
        _ada_kernel,
        grid=(depth, n // tn),
        in_specs=[
            pl.BlockSpec((MOD_ROWS, D_MODEL), lambda l, j: (0, 0)),
            pl.BlockSpec((None, D_MODEL, tn), lambda l, j: (l, 0, j)),
            pl.BlockSpec((None, 1, tn), lambda l, j: (l, 0, j)),
        ],
        out_specs=pl.BlockSpec((None, MOD_ROWS, tn), lambda l, j: (l, 0, j)),
        out_shape=jax.ShapeDtypeStruct((depth, MOD_ROWS, n), F32),
        compiler_params=_params(2),
        name="ada_mods",
    )(cond, ada_w, ada_b.reshape(depth, 1, n))


def _mod_spec(layer, chunk, n_grid_axes):
    if n_grid_axes == 1:
        return pl.BlockSpec((None, MOD_ROWS, D_MODEL), lambda i: (layer, 0, chunk))
    return pl.BlockSpec((None, MOD_ROWS, D_MODEL), lambda i, j: (layer, 0, chunk))


def _norm_matmul_kernel(x_ref, g_ref, sh_ref, sc_ref, w_ref, o_ref, h_scr, *, tiles_per_mod, mod_row0):
    i = pl.program_id(0)

    @pl.when(pl.program_id(1) == 0)
    def _():
        row = mod_row0 + i // tiles_per_mod
        h = _norm_mod(x_ref[...], g_ref[...], sh_ref[pl.ds(row, 1), :], sc_ref[pl.ds(row, 1), :])
        h_scr[...] = h.astype(BF16)

    o_ref[...] = _dot(h_scr[...], w_ref[...]).astype(o_ref.dtype)


def _norm_matmul(x2d, g, mods, layer, w, *, col0, n_cols, rows_per_mod, mod_row0, tm, tn, name,
                 cast_weights=()):
    m = x2d.shape[0]
    grid = (m // tm, n_cols // tn)
    tiles_per_mod = max(rows_per_mod // tm, 1)
    body = functools.partial(_norm_matmul_kernel, tiles_per_mod=tiles_per_mod, mod_row0=mod_row0)
    casts = [_Cast(cw, layer_, grid, col_tile) for cw, layer_, col_tile in cast_weights]
    cb0 = col0 // tn
    return pl.pallas_call(
        _with_casts(body, 5, 1, casts),
        grid=grid,
        in_specs=[
            pl.BlockSpec((tm, D_MODEL), lambda i, j: (i, 0)),
            pl.BlockSpec((None, 1, D_MODEL), lambda i, j: (layer, 0, 0)),
            _mod_spec(layer, 0, 2),
            _mod_spec(layer, 1, 2),
            pl.BlockSpec((D_MODEL, tn), lambda i, j: (0, cb0 + j)),
        ] + [c.in_spec for c in casts],
        out_specs=[pl.BlockSpec((tm, tn), lambda i, j: (i, j))] + [c.out_spec for c in casts],
        out_shape=[jax.ShapeDtypeStruct((m, n_cols), BF16)] + [c.out_shape for c in casts],
        scratch_shapes=[pltpu.VMEM((tm, D_MODEL), BF16)],
        compiler_params=_params(2),
        name=name,
    )(x2d, g, mods, mods, w, *[c.w for c in casts])


def _bias_kernel(rpb_ref, bias_ref, cap_ref):
    h = pl.program_id(0)
    n_dc = 2 * WIN_COLS - 1
    n_dr = 2 * WIN_ROWS - 1
    q = lax.broadcasted_iota(jnp.int32, (GRID_W, 2 * GRID_W), 0)
    lane = lax.broadcasted_iota(jnp.int32, (GRID_W, 2 * GRID_W), 1)
    second = lane >= GRID_W
    k = jnp.where(second, lane - GRID_W, lane)
    dc = jnp.clip(k - q, -(WIN_COLS - 1), WIN_COLS - 1) + (WIN_COLS - 1)
    pairs = []
    for m in range(n_dr - 1):
        acc = jnp.zeros((GRID_W, 2 * GRID_W), F32)
        for d in range(n_dc):
            val = jnp.where(second, rpb_ref[h, (m + 1) * n_dc + d], rpb_ref[h, m * n_dc + d])
            acc = jnp.where(dc == d, val, acc)
        pairs.append(acc)
    for o in range(WIN_ROWS):
        for jj in range(WIN_ROWS // 2):
            bias_ref[o, :, jj * 2 * GRID_W:(jj + 1) * 2 * GRID_W] = pairs[o + 2 * jj]
    col_start = jnp.clip(q - WIN_COLS // 2, 0, GRID_W - WIN_COLS)
    cap = jnp.where((k >= col_start) & (k < col_start + WIN_COLS), UNMASKED_CAP, NEG_INF).astype(F32)
    for jj in range(WIN_ROWS // 2):
        cap_ref[:, jj * 2 * GRID_W:(jj + 1) * 2 * GRID_W] = cap


def _bias_tables(rpb):
    h = rpb.shape[0]
    kw = WIN_ROWS * GRID_W
    return pl.pallas_call(
        _bias_kernel,
        grid=(h,),
        in_specs=[pl.BlockSpec(memory_space=pltpu.SMEM)],
        out_specs=[
            pl.BlockSpec((None, WIN_ROWS, GRID_W, kw), lambda i: (i, 0, 0, 0)),
            pl.BlockSpec((GRID_W, kw), lambda i: (0, 0)),
        ],
        out_shape=[
            jax.ShapeDtypeStruct((h, WIN_ROWS, GRID_W, kw), F32),
            jax.ShapeDtypeStruct((GRID_W, kw), F32),
        ],
        compiler_params=_params(1),
        name="bias_tables",
    )(rpb.reshape(h, -1))


def _attn_kernel(q_ref, k_ref, v_ref, kc_ref, vc_ref, bias_ref, cap_ref, o_ref,
                 s_scr, e_scr, vx_scr, vcx_scr, acc_scr, *, rows):
    scale = HEAD_DIM ** -0.5
    kw = WIN_ROWS * GRID_W
    ctx_len = kc_ref.shape[0]
    nt = (((1,), (1,)), ((), ()))

    vx_scr[:, :HEAD_DIM] = v_ref[...]
    vx_scr[:, HEAD_DIM:] = jnp.ones((v_ref.shape[0], HEAD_DIM), BF16)
    vcx_scr[:, :HEAD_DIM] = vc_ref[...]
    vcx_scr[:, HEAD_DIM:] = jnp.ones((ctx_len, HEAD_DIM), BF16)

    s_scr[:, kw:] = lax.dot_general(q_ref[...], kc_ref[...], nt, preferred_element_type=F32) * scale

    def scores(r, carry):
        row_start = jnp.clip(r - WIN_ROWS // 2, 0, rows - WIN_ROWS)
        o = row_start - r + (WIN_ROWS - 1)
        q0 = pl.multiple_of(r * GRID_W, GRID_W)
        k0 = pl.multiple_of(row_start * GRID_W, GRID_W)
        s = lax.dot_general(q_ref[pl.ds(q0, GRID_W), :], k_ref[pl.ds(k0, kw), :], nt,
                            preferred_element_type=F32)
        s_scr[pl.ds(q0, GRID_W), :kw] = jnp.minimum(s * scale + bias_ref[o], cap_ref[...])
        return carry

    lax.fori_loop(0, rows, scores, 0, unroll=16)

    def expo(r, carry):
        q0 = pl.multiple_of(r * GRID_W, GRID_W)
        s = s_scr[pl.ds(q0, GRID_W), :]
        e_scr[pl.ds(q0, GRID_W), :] = jnp.exp(s - jnp.max(s, axis=-1, keepdims=True)).astype(BF16)
        return carry

    lax.fori_loop(0, rows, expo, 0, unroll=8)

    acc_scr[...] = _dot(e_scr[:, kw:], vcx_scr[...])

    def values(r, carry):
        row_start = jnp.clip(r - WIN_ROWS // 2, 0, rows - WIN_ROWS)
        q0 = pl.multiple_of(r * GRID_W, GRID_W)
        k0 = pl.multiple_of(row_start * GRID_W, GRID_W)
        t = acc_scr[pl.ds(q0, GRID_W), :] + _dot(e_scr[pl.ds(q0, GRID_W), :kw], vx_scr[pl.ds(k0, kw), :])
        o_ref[pl.ds(q0, GRID_W), :] = (t[:, :HEAD_DIM] / t[:, HEAD_DIM:]).astype(o_ref.dtype)
        return carry

    lax.fori_loop(0, rows, values, 0, unroll=16)


def _attention(qkv, ctx_kv, bias, cap, batch, seq, ctx_len, cast_weights=()):
    rows = seq // GRID_W
    kw = WIN_ROWS * GRID_W
    grid = (NA_HEADS, batch)
    casts = [_Cast(cw, layer_, grid, col_tile) for cw, layer_, col_tile in cast_weights]
    return pl.pallas_call(
        _with_casts(functools.partial(_attn_kernel, rows=rows), 7, 1, casts),
        grid=grid,
        in_specs=[
            pl.BlockSpec((seq, HEAD_DIM), lambda h, b: (b, h)),
            pl.BlockSpec((seq, HEAD_DIM), lambda h, b: (b, NA_HEADS + h)),
            pl.BlockSpec((seq, HEAD_DIM), lambda h, b: (b, 2 * NA_HEADS + h)),
            pl.BlockSpec((ctx_len, HEAD_DIM), lambda h, b: (b, h)),
            pl.BlockSpec((ctx_len, HEAD_DIM), lambda h, b: (b, NA_HEADS + h)),
            pl.BlockSpec((None, WIN_ROWS, GRID_W, kw), lambda h, b: (h, 0, 0, 0)),
            pl.BlockSpec((GRID_W, kw), lambda h, b: (0, 0)),
        ] + [c.in_spec for c in casts],
        out_specs=[pl.BlockSpec((seq, HEAD_DIM), lambda h, b: (b, h))] + [c.out_spec for c in casts],
        out_shape=[jax.ShapeDtypeStruct((batch * seq, NA_WIDTH), BF16)] + [c.out_shape for c in casts],
        scratch_shapes=[
            pltpu.VMEM((seq, kw + ctx_len), F32),
            pltpu.VMEM((seq, kw + ctx_len), BF16),
            pltpu.VMEM((seq, 2 * HEAD_DIM), BF16),
            pltpu.VMEM((ctx_len, 2 * HEAD_DIM), BF16),
            pltpu.VMEM((seq, 2 * HEAD_DIM), F32),
        ],
        compiler_params=_params(2),
        name="nbr_attention",
    )(qkv, qkv, qkv, ctx_kv, ctx_kv, bias, cap, *[c.w for c in casts])


def _residual_and_ffn_norm(rows, mod_row, x_ref, mix, gate_ref, gn_ref, shf_ref, scf_ref, x1_ref, h_ref):
    x1 = x_ref[rows, :] + gate_ref[pl.ds(mod_row, 1), :] * mix
    x1_ref[rows, :] = x1
    h = _norm_mod(x1, gn_ref[...], shf_ref[pl.ds(mod_row, 1), :], scf_ref[pl.ds(mod_row, 1), :])
    h_ref[rows, :] = h.astype(BF16)


def _row_chunks(tm):
    return [slice(r, r + ROW_CHUNK) for r in range(0, tm, ROW_CHUNK)]


def _skewed(chunks, matmul_fn, epilogue_fn):
    pending = None
    for rows in chunks:
        acc = matmul_fn(rows)
        if pending is not None:
            epilogue_fn(*pending)
        pending = (rows, acc)
    epilogue_fn(*pending)


def _even_out_kernel(attn_ref, b_ref, c_ref, u_ref, cp_ref, up_ref, cn_ref, un_ref, cw_ref, wo_ref,
                     x_ref, gate_ref, gn_ref, shf_ref, scf_ref, x1_ref, h_ref, conv_scr, *, tiles_per_mod, halo):
    i = pl.program_id(0)
    tm = c_ref.shape[0]
    t = i % tiles_per_mod
    chunks = _row_chunks(tm)
    mix_attn = [_dot(attn_ref[rows, :], wo_ref[0:NA_WIDTH, :]) for rows in chunks]
    z = c_ref[...].astype(F32) * u_ref[...].astype(F32)
    z_before = cp_ref[halo - 1:halo, :].astype(F32) * up_ref[halo - 1:halo, :].astype(F32)
    z_after = cn_ref[0:1, :].astype(F32) * un_ref[0:1, :].astype(F32)
    z_before = jnp.where(t == 0, 0.0, z_before)
    z_after = jnp.where(t == tiles_per_mod - 1, 0.0, z_after)
    ridx = lax.broadcasted_iota(jnp.int32, (tm, 1), 0)
    z_prev = jnp.where(ridx == 0, z_before, pltpu.roll(z, 1, 0))
    z_next = jnp.where(ridx == tm - 1, z_after, pltpu.roll(z, tm - 1, 0))
    y = z_prev * cw_ref[0:1, :] + z * cw_ref[1:2, :] + z_next * cw_ref[2:3, :]
    conv_scr[...] = (b_ref[...].astype(F32) * y).astype(BF16)
    _skewed(
        chunks,
        lambda rows: mix_attn[chunks.index(rows)] + _dot(conv_scr[rows, :], wo_ref[NA_WIDTH:, :]),
        lambda rows, mix: _residual_and_ffn_norm(rows, i // tiles_per_mod, x_ref, mix, gate_ref, gn_ref,
                                                 shf_ref, scf_ref, x1_ref, h_ref))


def _even_out(attn, proj, conv_w, w_out, x2d, mods, norm_ffn_g, layer, seq, tm):
    m = x2d.shape[0]
    halo = 16
    hb = tm // halo
    n_halo_blocks = m // halo
    cb = NA_WIDTH * 3 // CONV_CH
    kern = functools.partial(_even_out_kernel, tiles_per_mod=seq // tm, halo=halo)
    main = lambda c: pl.BlockSpec((tm, CONV_CH), lambda i: (i, c))
    prev = lambda c: pl.BlockSpec((halo, CONV_CH), lambda i: (jnp.maximum(i * hb - 1, 0), c))
    nxt = lambda c: pl.BlockSpec((halo, CONV_CH), lambda i: (jnp.minimum((i + 1) * hb, n_halo_blocks - 1), c))
    return pl.pallas_call(
        kern,
        grid=(m // tm,),
        in_specs=[
            pl.BlockSpec((tm, NA_WIDTH), lambda i: (i, 0)),
            main(cb), main(cb + 1), main(cb + 2),
            prev(cb + 1), prev(cb + 2), nxt(cb + 1), nxt(cb + 2),
            _resident_spec(conv_w.shape),
            _resident_spec(w_out.shape),
            pl.BlockSpec((tm, D_MODEL), lambda i: (i, 0)),
            _mod_spec(layer, 2, 1),
            pl.BlockSpec((None, 1, D_MODEL), lambda i: (layer, 0, 0)),
            _mod_spec(layer, 3, 1),
            _mod_spec(layer, 4, 1),
        ],
        out_specs=[
            pl.BlockSpec((tm, D_MODEL), lambda i: (i, 0)),
            pl.BlockSpec((tm, D_MODEL), lambda i: (i, 0)),
        ],
        out_shape=[
            jax.ShapeDtypeStruct((m, D_MODEL), F32),
            jax.ShapeDtypeStruct((m, D_MODEL), BF16),
        ],
        scratch_shapes=[pltpu.VMEM((tm, CONV_CH), BF16)],
        compiler_params=_params(1),
        name="even_out",
    )(attn, proj, proj, proj, proj, proj, proj, proj, conv_w, w_out, x2d, mods, norm_ffn_g, mods, mods)


def _ffn_kernel(*refs, tiles_per_mod, modulated):
    if modulated:
        (h_ref, wg_ref, wu_ref, wd_ref, x_ref, gate_ref, gn_ref, sh_ref, sc_ref,
         x2_ref, y_ref, acc_ref) = refs
    else:
        h_ref, wg_ref, wu_ref, wd_ref, x_ref, gate_ref, gn_ref, y_ref, acc_ref = refs
    i = pl.program_id(0)
    f = pl.program_id(1)

    @pl.when(f == 0)
    def _():
        acc_ref[...] = jnp.zeros_like(acc_ref)

    tm = h_ref.shape[0]
    rc = tm // FFN_ROW_CHUNKS
    ts = []
    for c in range(FFN_ROW_CHUNKS):
        h = h_ref[c * rc:(c + 1) * rc, :]
        ts.append((jax.nn.silu(_dot(h, wg_ref[...])) * _dot(h, wu_ref[...])).astype(BF16))
    for c in range(FFN_ROW_CHUNKS):
        acc_ref[c * rc:(c + 1) * rc, :] += _dot(ts[c], wd_ref[...])

    @pl.when(f == pl.num_programs(1) - 1)
    def _():
        row = i // tiles_per_mod
        x2 = x_ref[...] + gate_ref[pl.ds(row, 1), :] * acc_ref[...]
        if modulated:
            x2_ref[...] = x2
            y = _norm_mod(x2, gn_ref[...], sh_ref[pl.ds(row, 1), :], sc_ref[pl.ds(row, 1), :])
        else:
            y = x2 * lax.rsqrt(jnp.mean(x2 * x2, axis=-1, keepdims=True) + EPS) * gn_ref[...]
        y_ref[...] = y.astype(y_ref.dtype)


def _ffn(h, w_gate, w_up, w_down, x2d, mods, layer, next_g, next_layer, seq, tm):
    m = x2d.shape[0]
    n_tiles, _, tf = w_gate.shape
    hidden = n_tiles * tf
    modulated = next_layer is not None
    kern = functools.partial(_ffn_kernel, tiles_per_mod=seq // tm, modulated=modulated)
    row_spec = pl.BlockSpec((tm, D_MODEL), lambda i, f: (i, 0))
    in_specs = [
        row_spec,
        pl.BlockSpec((None, D_MODEL, tf), lambda i, f: (f, 0, 0)),
        pl.BlockSpec((None, D_MODEL, tf), lambda i, f: (f, 0, 0)),
        pl.BlockSpec((tf, D_MODEL), lambda i, f: (f, 0)),
        row_spec,
        _mod_spec(layer, 5, 2),
    ]
    args = [h, w_gate, w_up, w_down, x2d, mods]
    if modulated:
        in_specs += [pl.BlockSpec((None, 1, D_MODEL), lambda i, f: (next_layer, 0, 0)),
                     _mod_spec(next_layer, 0, 2), _mod_spec(next_layer, 1, 2)]
        args += [next_g, mods, mods]
        out_specs = [row_spec, row_spec]
        out_shape = [jax.ShapeDtypeStruct((m, D_MODEL), F32), jax.ShapeDtypeStruct((m, D_MODEL), BF16)]
    else:
        in_specs += [pl.BlockSpec((1, D_MODEL), lambda i, f: (0, 0))]
        args += [next_g]
        out_specs = row_spec
        out_shape = jax.ShapeDtypeStruct((m, D_MODEL), F32)
    return pl.pallas_call(
        kern,
        grid=(m // tm, hidden // tf),
        in_specs=in_specs,
        out_specs=out_specs,
        out_shape=out_shape,
        scratch_shapes=[pltpu.VMEM((tm, D_MODEL), F32)],
        compiler_params=_params(2),
        name="ffn_final" if not modulated else "ffn",
    )(*args)


def _odd_in_kernel(h_ref, w_ref, lg_ref, lb_ref, u_ref, v_ref):
    def matmuls(rows):
        h = h_ref[rows, :]
        return _dot(h, w_ref[:, :D_MODEL]), _dot(h, w_ref[:, D_MODEL:])

    def epilogue(rows, acc):
        u_ref[rows, :] = jax.nn.gelu(acc[0], approximate=True).astype(BF16)
        z = jax.nn.gelu(acc[1], approximate=True)
        zc = z - jnp.mean(z, axis=-1, keepdims=True)
        var = jnp.mean(zc * zc, axis=-1, keepdims=True)
        v_ref[rows, :] = (zc * lax.rsqrt(var + EPS) * lg_ref[...] + lb_ref[...]).astype(BF16)

    _skewed(_row_chunks(h_ref.shape[0]), matmuls, epilogue)


def _odd_in(h, w_in, ln_g, ln_b, tm):
    m = h.shape[0]
    row_spec = pl.BlockSpec((tm, D_MODEL), lambda i: (i, 0))
    return pl.pallas_call(
        _odd_in_kernel,
        grid=(m // tm,),
        in_specs=[
            row_spec,
            _resident_spec(w_in.shape),
            pl.BlockSpec((1, D_MODEL), lambda i: (0, 0)),
            pl.BlockSpec((1, D_MODEL), lambda i: (0, 0)),
        ],
        out_specs=[row_spec, row_spec],
        out_shape=[jax.ShapeDtypeStruct((m, D_MODEL), BF16)] * 2,
        compiler_params=_params(1),
        name="odd_in",
    )(h, w_in, ln_g, ln_b)


def _odd_out_kernel(u_ref, v_ref, ws_ref, bs_ref, wo_ref, x_ref, gate_ref, gn_ref, shf_ref, scf_ref,
                    x1_ref, h_ref, t_scr, *, tiles_per_mod):
    i = pl.program_id(0)
    n_chunks = u_ref.shape[0] // CHUNK
    for g in range(SG_GROUPS):
        cols = slice(g * CHUNK, (g + 1) * CHUNK)
        vcat = jnp.concatenate([v_ref[n * CHUNK:(n + 1) * CHUNK, cols] for n in range(n_chunks)], axis=1)
        mixed = _dot(ws_ref[g], vcat)
        for n in range(n_chunks):
            rows = slice(n * CHUNK, (n + 1) * CHUNK)
            gated = u_ref[rows, cols].astype(F32) * (mixed[:, n * CHUNK:(n + 1) * CHUNK] + bs_ref[g])
            t_scr[rows, cols] = gated.astype(BF16)
    _skewed(
        _row_chunks(u_ref.shape[0]),
        lambda rows: _dot(t_scr[rows, :], wo_ref[...]),
        lambda rows, mix: _residual_and_ffn_norm(rows, i // tiles_per_mod, x_ref, mix, gate_ref, gn_ref,
                                                 shf_ref, scf_ref, x1_ref, h_ref))


def _odd_out(u, v, w_s, b_s, w_out, x2d, mods, norm_ffn_g, layer, seq, tm):
    m = x2d.shape[0]
    kern = functools.partial(_odd_out_kernel, tiles_per_mod=seq // tm)
    return pl.pallas_call(
        kern,
        grid=(m // tm,),
        in_specs=[
            pl.BlockSpec((tm, D_MODEL), lambda i: (i, 0)),
            pl.BlockSpec((tm, D_MODEL), lambda i: (i, 0)),
            _resident_spec(w_s.shape),
            _resident_spec(b_s.shape),
            _resident_spec(w_out.shape),
            pl.BlockSpec((tm, D_MODEL), lambda i: (i, 0)),
            _mod_spec(layer, 2, 1),
            pl.BlockSpec((None, 1, D_MODEL), lambda i: (layer, 0, 0)),
            _mod_spec(layer, 3, 1),
            _mod_spec(layer, 4, 1),
        ],
        out_specs=[
            pl.BlockSpec((tm, D_MODEL), lambda i: (i, 0)),
            pl.BlockSpec((tm, D_MODEL), lambda i: (i, 0)),
        ],
        out_shape=[
            jax.ShapeDtypeStruct((m, D_MODEL), F32),
            jax.ShapeDtypeStruct((m, D_MODEL), BF16),
        ],
        scratch_shapes=[pltpu.VMEM((tm, D_MODEL), BF16)],
        compiler_params=_params(1),
        name="odd_out",
    )(u, v, w_s, b_s, w_out, x2d, mods, norm_ffn_g, mods, mods)


def kernel(x, c, ctx, c_ctx, ada_w, ada_b, norm_mix_g, norm_ffn_g, ffn_w_gate, ffn_w_up, ffn_w_down,
           even_w_in, even_w_out, rpb, conv_w, odd_w_in, odd_w_out, sg_ln_g, sg_ln_b, sg_w, sg_b,
           final_norm_g):
    batch, seq, d = x.shape
    ctx_len = ctx.shape[1]
    depth = ada_w.shape[0]
    assert d == D_MODEL and depth == 2 and batch < MOD_ROWS and seq % (GRID_W * WIN_ROWS) == 0

    bf = lambda w: w.astype(BF16)
    x2d = x.reshape(batch * seq, d)
    ctx2d = ctx.reshape(batch * ctx_len, d)
    g_mix = norm_mix_g.reshape(depth, 1, d)
    g_ffn = norm_ffn_g.reshape(depth, 1, d)

    cond = jnp.concatenate([c, c_ctx[None], jnp.zeros((MOD_ROWS - batch - 1, d), F32)], axis=0)
    mods = _ada_mods(cond, ada_w, ada_b)

    ffn_casts = lambda layer: [(ffn_w_gate, layer, FFN_TILE), (ffn_w_up, layer, FFN_TILE),
                               (ffn_w_down, layer, None)]
    w_in0 = bf(even_w_in[0])
    proj, w_out0, wg0, wu0, wd0 = _norm_matmul(
        x2d, g_mix, mods, 0, w_in0, col0=0, n_cols=w_in0.shape[1], rows_per_mod=seq, mod_row0=0,
        tm=1024, tn=1024, name="even_in", cast_weights=[(even_w_out, 0, None)] + ffn_casts(0))
    ctx_kv, = _norm_matmul(ctx2d, g_mix, mods, 0, w_in0, col0=NA_WIDTH, n_cols=2 * NA_WIDTH,
                           rows_per_mod=batch * ctx_len, mod_row0=batch, tm=1024, tn=1024, name="ctx_kv")
    bias, cap = _bias_tables(rpb[0])
    attn, w_in1, w_out1, wg1, wu1, wd1 = _attention(
        proj, ctx_kv, bias, cap, batch, seq, ctx_len,
        cast_weights=[(odd_w_in, 0, None), (odd_w_out, 0, None)] + ffn_casts(1))
    x1, h = _even_out(attn, proj, conv_w[0], w_out0, x2d, mods, g_ffn, 0, seq, tm=512)
    x2, h = _ffn(h, wg0, wu0, wd0, x1, mods, 0, g_mix, 1, seq, tm=512)

    u, v = _odd_in(h, w_in1, sg_ln_g[0][None], sg_ln_b[0][None], tm=512)
    b_s = jnp.broadcast_to(sg_b[0][:, :, None], (SG_GROUPS, CHUNK, CHUNK))
    x1, h = _odd_out(u, v, bf(sg_w[0]), b_s, w_out1, x2, mods, g_ffn, 1, seq, tm=512)
    out = _ffn(h, wg1, wu1, wd1, x1, mods, 1, final_norm_g[None], None, seq, tm=512)
    return out.reshape(batch, seq, d)
```

```python
import functools

import jax
import jax.numpy as jnp
from jax import lax
from jax.experimental import pallas as pl
from jax.experimental.pallas import tpu as pltpu

D_MODEL = 2048
GRID_W = 64
NA_HEADS = 8
HEAD_DIM = 128
NA_WIDTH = NA_HEADS * HEAD_DIM
CONV_CH = D_MODEL - NA_WIDTH
WIN_ROWS = 8
WIN_COLS = 16
CHUNK = 128
SG_GROUPS = 16
EPS = 1e-6
NEG_INF = -1e30
UNMASKED_CAP = 3e38
MOD_ROWS = 16
FFN_TILE = 512
FFN_ROW_CHUNKS = 2
FFN_EPILOGUE_SLICES = 8
ROW_CHUNK = 256

F32 = jnp.float32
BF16 = jnp.bfloat16
BF16_SUBLANES = 16

VMEM_LIMIT_BYTES = 56 * 1024 * 1024


def _params(n_axes):
    return pltpu.CompilerParams(dimension_semantics=("arbitrary",) * n_axes,
                                vmem_limit_bytes=VMEM_LIMIT_BYTES)


def _resident_spec(shape):
    zeros = (0,) * len(shape)
    return pl.BlockSpec(shape, lambda *_: zeros, pipeline_mode=pl.Buffered(1))


def _dot(a, b):
    return jnp.dot(a, b, preferred_element_type=F32)


def _norm_mod(x, g, shift, scale):
    y = x * lax.rsqrt(jnp.mean(x * x, axis=-1, keepdims=True) + EPS)
    return (y * g) * (1.0 + scale) + shift


def _linear_step(ids, grid):
    step = ids[0]
    for axis in range(1, len(grid)):
        step = step * grid[axis] + ids[axis]
    return step


class _Cast:
    def __init__(self, stacked_w, layer, grid, col_tile=None):
        _, rows, cols = stacked_w.shape
        n_steps = 1
        for g in grid:
            n_steps *= g
        self.w = stacked_w
        self.grid = grid
        self.col_tile = col_tile
        self.n_blocks = nb = max(n for n in range(1, n_steps + 1)
                                 if rows % n == 0 and (rows // n) % BF16_SUBLANES == 0)
        rb = rows // nb
        block = lambda *ids: jnp.minimum(_linear_step(ids, grid), nb - 1)
        self.in_spec = pl.BlockSpec((None, rb, cols), lambda *ids: (layer, block(*ids), 0))
        if col_tile is None:
            self.out_shape = jax.ShapeDtypeStruct((rows, cols), BF16)
            self.out_spec = pl.BlockSpec((rb, cols), lambda *ids: (block(*ids), 0))
        else:
            nt = cols // col_tile
            self.out_shape = jax.ShapeDtypeStruct((nt, rows, col_tile), BF16)
            self.out_spec = pl.BlockSpec((nt, rb, col_tile), lambda *ids: (0, block(*ids), 0))

    def run(self, w_ref, o_ref):
        step = _linear_step([pl.program_id(a) for a in range(len(self.grid))], self.grid)

        @pl.when(step < self.n_blocks)
        def _():
            if self.col_tile is None:
                o_ref[...] = w_ref[...].astype(BF16)
            else:
                for t in range(o_ref.shape[0]):
                    o_ref[t] = w_ref[:, t * self.col_tile:(t + 1) * self.col_tile].astype(BF16)


def _with_casts(body, n_in, n_out, casts):
    n_c = len(casts)

    def kern(*refs):
        ins, rest = refs[:n_in], refs[n_in:]
        cast_ins, rest = rest[:n_c], rest[n_c:]
        outs, rest = rest[:n_out], rest[n_out:]
        cast_outs, scratch = rest[:n_c], rest[n_c:]
        body(*ins, *outs, *scratch)
        for cast, w_ref, o_ref in zip(casts, cast_ins, cast_outs):
            cast.run(w_ref, o_ref)

    return kern


def _ada_kernel(cond_ref, w_ref, b_ref, o_ref):
    a = jax.nn.silu(cond_ref[...]).astype(BF16)
    o_ref[...] = _dot(a, w_ref[...].astype(BF16)) + b_ref[...]


def _ada_mods(cond, ada_w, ada_b):
    depth, _, n = ada_w.shape
    tn = 1024
    return pl.pallas_call(
        _ada_kernel,
        grid=(depth, n // tn),
        in_specs=[
            pl.BlockSpec((MOD_ROWS, D_MODEL), lambda l, j: (0, 0)),
            pl.BlockSpec((None, D_MODEL, tn), lambda l, j: (l, 0, j)),
            pl.BlockSpec((None, 1, tn), lambda l, j: (l, 0, j)),
        ],
        out_specs=pl.BlockSpec((None, MOD_ROWS, tn), lambda l, j: (l, 0, j)),
        out_shape=jax.ShapeDtypeStruct((depth, MOD_ROWS, n), F32),
        compiler_params=_params(2),
        name="ada_mods",
    )(cond, ada_w, ada_b.reshape(depth, 1, n))


def _mod_spec(layer, chunk, n_grid_axes):
    if n_grid_axes == 1:
        return pl.BlockSpec((None, MOD_ROWS, D_MODEL), lambda i: (layer, 0, chunk))
    return pl.BlockSpec((None, MOD_ROWS, D_MODEL), lambda i, j: (layer, 0, chunk))


def _norm_matmul_kernel(x_ref, g_ref, sh_ref, sc_ref, w_ref, o_ref, h_scr, *, tiles_per_mod, mod_row0):
    i = pl.program_id(0)

    @pl.when(pl.program_id(1) == 0)
    def _():
        row = mod_row0 + i // tiles_per_mod
        h = _norm_mod(x_ref[...], g_ref[...], sh_ref[pl.ds(row, 1), :], sc_ref[pl.ds(row, 1), :])
        h_scr[...] = h.astype(BF16)

    o_ref[...] = _dot(h_scr[...], w_ref[...]).astype(o_ref.dtype)


def _norm_matmul(x2d, g, mods, layer, w, *, col0, n_cols, rows_per_mod, mod_row0, tm, tn, name,
                 cast_weights=()):
    m = x2d.shape[0]
    grid = (m // tm, n_cols // tn)
    tiles_per_mod = max(rows_per_mod // tm, 1)
    body = functools.partial(_norm_matmul_kernel, tiles_per_mod=tiles_per_mod, mod_row0=mod_row0)
    casts = [_Cast(cw, layer_, grid, col_tile) for cw, layer_, col_tile in cast_weights]
    cb0 = col0 // tn
    return pl.pallas_call(
        _with_casts(body, 5, 1, casts),
        grid=grid,
        in_specs=[
            pl.BlockSpec((tm, D_MODEL), lambda i, j: (i, 0)),
            pl.BlockSpec((None, 1, D_MODEL), lambda i, j: (layer, 0, 0)),
            _mod_spec(layer, 0, 2),
            _mod_spec(layer, 1, 2),
            pl.BlockSpec((D_MODEL, tn), lambda i, j: (0, cb0 + j)),
        ] + [c.in_spec for c in casts],
        out_specs=[pl.BlockSpec((tm, tn), lambda i, j: (i, j))] + [c.out_spec for c in casts],
        out_shape=[jax.ShapeDtypeStruct((m, n_cols), BF16)] + [c.out_shape for c in casts],
        scratch_shapes=[pltpu.VMEM((tm, D_MODEL), BF16)],
        compiler_params=_params(2),
        name=name,
    )(x2d, g, mods, mods, w, *[c.w for c in casts])


def _bias_kernel(rpb_ref, bias_ref, cap_ref):
    h = pl.program_id(0)
    n_dc = 2 * WIN_COLS - 1
    n_dr = 2 * WIN_ROWS - 1
    q = lax.broadcasted_iota(jnp.int32, (GRID_W, 2 * GRID_W), 0)
    lane = lax.broadcasted_iota(jnp.int32, (GRID_W, 2 * GRID_W), 1)
    second = lane >= GRID_W
    k = jnp.where(second, lane - GRID_W, lane)
    dc = jnp.clip(k - q, -(WIN_COLS - 1), WIN_COLS - 1) + (WIN_COLS - 1)
    pairs = []
    for m in range(n_dr - 1):
        acc = jnp.zeros((GRID_W, 2 * GRID_W), F32)
        for d in range(n_dc):
            val = jnp.where(second, rpb_ref[h, (m + 1) * n_dc + d], rpb_ref[h, m * n_dc + d])
            acc = jnp.where(dc == d, val, acc)
        pairs.append(acc)
    for o in range(WIN_ROWS):
        for jj in range(WIN_ROWS // 2):
            bias_ref[o, :, jj * 2 * GRID_W:(jj + 1) * 2 * GRID_W] = pairs[o + 2 * jj]
    col_start = jnp.clip(q - WIN_COLS // 2, 0, GRID_W - WIN_COLS)
    cap = jnp.where((k >= col_start) & (k < col_start + WIN_COLS), UNMASKED_CAP, NEG_INF).astype(F32)
    for jj in range(WIN_ROWS // 2):
        cap_ref[:, jj * 2 * GRID_W:(jj + 1) * 2 * GRID_W] = cap


def _bias_tables(rpb):
    h = rpb.shape[0]
    kw = WIN_ROWS * GRID_W
    return pl.pallas_call(
        _bias_kernel,
        grid=(h,),
        in_specs=[pl.BlockSpec(memory_space=pltpu.SMEM)],
        out_specs=[
            pl.BlockSpec((None, WIN_ROWS, GRID_W, kw), lambda i: (i, 0, 0, 0)),
            pl.BlockSpec((GRID_W, kw), lambda i: (0, 0)),
        ],
        out_shape=[
            jax.ShapeDtypeStruct((h, WIN_ROWS, GRID_W, kw), F32),
            jax.ShapeDtypeStruct((GRID_W, kw), F32),
        ],
        compiler_params=_params(1),
        name="bias_tables",
    )(rpb.reshape(h, -1))


def _attn_kernel(q_ref, k_ref, v_ref, kc_ref, vc_ref, bias_ref, cap_ref, o_ref,
                 s_scr, e_scr, vx_scr, vcx_scr, acc_scr, *, rows):
    scale = HEAD_DIM ** -0.5
    kw = WIN_ROWS * GRID_W
    ctx_len = kc_ref.shape[0]
    nt = (((1,), (1,)), ((), ()))

    vx_scr[:, :HEAD_DIM] = v_ref[...]
    vx_scr[:, HEAD_DIM:] = jnp.ones((v_ref.shape[0], HEAD_DIM), BF16)
    vcx_scr[:, :HEAD_DIM] = vc_ref[...]
    vcx_scr[:, HEAD_DIM:] = jnp.ones((ctx_len, HEAD_DIM), BF16)

    s_scr[:, kw:] = lax.dot_general(q_ref[...], kc_ref[...], nt, preferred_element_type=F32) * scale

    def scores(r, carry):
        row_start = jnp.clip(r - WIN_ROWS // 2, 0, rows - WIN_ROWS)
        o = row_start - r + (WIN_ROWS - 1)
        q0 = pl.multiple_of(r * GRID_W, GRID_W)
        k0 = pl.multiple_of(row_start * GRID_W, GRID_W)
        s = lax.dot_general(q_ref[pl.ds(q0, GRID_W), :], k_ref[pl.ds(k0, kw), :], nt,
                            preferred_element_type=F32)
        s_scr[pl.ds(q0, GRID_W), :kw] = jnp.minimum(s * scale + bias_ref[o], cap_ref[...])
        return carry

    lax.fori_loop(0, rows, scores, 0, unroll=16)

    def expo(r, carry):
        q0 = pl.multiple_of(r * GRID_W, GRID_W)
        s = s_scr[pl.ds(q0, GRID_W), :]
        e_scr[pl.ds(q0, GRID_W), :] = jnp.exp(s - jnp.max(s, axis=-1, keepdims=True)).astype(BF16)
        return carry

    lax.fori_loop(0, rows, expo, 0, unroll=8)

    acc_scr[...] = _dot(e_scr[:, kw:], vcx_scr[...])

    def values(r, carry):
        row_start = jnp.clip(r - WIN_ROWS // 2, 0, rows - WIN_ROWS)
        q0 = pl.multiple_of(r * GRID_W, GRID_W)
        k0 = pl.multiple_of(row_start * GRID_W, GRID_W)
        t = acc_scr[pl.ds(q0, GRID_W), :] + _dot(e_scr[pl.ds(q0, GRID_W), :kw], vx_scr[pl.ds(k0, kw), :])
        o_ref[pl.ds(q0, GRID_W), :] = (t[:, :HEAD_DIM] / t[:, HEAD_DIM:]).astype(o_ref.dtype)
        return carry

    lax.fori_loop(0, rows, values, 0, unroll=16)


def _attention(qkv, ctx_kv, bias, cap, batch, seq, ctx_len, cast_weights=()):
    rows = seq // GRID_W
    kw = WIN_ROWS * GRID_W
    grid = (NA_HEADS, batch)
    casts = [_Cast(cw, layer_, grid, col_tile) for cw, layer_, col_tile in cast_weights]
    return pl.pallas_call(
        _with_casts(functools.partial(_attn_kernel, rows=rows), 7, 1, casts),
        grid=grid,
        in_specs=[
            pl.BlockSpec((seq, HEAD_DIM), lambda h, b: (b, h)),
            pl.BlockSpec((seq, HEAD_DIM), lambda h, b: (b, NA_HEADS + h)),
            pl.BlockSpec((seq, HEAD_DIM), lambda h, b: (b, 2 * NA_HEADS + h)),
            pl.BlockSpec((ctx_len, HEAD_DIM), lambda h, b: (b, h)),
            pl.BlockSpec((ctx_len, HEAD_DIM), lambda h, b: (b, NA_HEADS + h)),
            pl.BlockSpec((None, WIN_ROWS, GRID_W, kw), lambda h, b: (h, 0, 0, 0)),
            pl.BlockSpec((GRID_W, kw), lambda h, b: (0, 0)),
        ] + [c.in_spec for c in casts],
        out_specs=[pl.BlockSpec((seq, HEAD_DIM), lambda h, b: (b, h))] + [c.out_spec for c in casts],
        out_shape=[jax.ShapeDtypeStruct((batch * seq, NA_WIDTH), BF16)] + [c.out_shape for c in casts],
        scratch_shapes=[
            pltpu.VMEM((seq, kw + ctx_len), F32),
            pltpu.VMEM((seq, kw + ctx_len), BF16),
            pltpu.VMEM((seq, 2 * HEAD_DIM), BF16),
            pltpu.VMEM((ctx_len, 2 * HEAD_DIM), BF16),
            pltpu.VMEM((seq, 2 * HEAD_DIM), F32),
        ],
        compiler_params=_params(2),
        name="nbr_attention",
    )(qkv, qkv, qkv, ctx_kv, ctx_kv, bias, cap, *[c.w for c in casts])


def _residual_and_ffn_norm(rows, mod_row, x_ref, mix, gate_ref, gn_ref, shf_ref, scf_ref, x1_ref, h_ref):
    x1 = x_ref[rows, :] + gate_ref[pl.ds(mod_row, 1), :] * mix
    x1_ref[rows, :] = x1
    h = _norm_mod(x1, gn_ref[...], shf_ref[pl.ds(mod_row, 1), :], scf_ref[pl.ds(mod_row, 1), :])
    h_ref[rows, :] = h.astype(BF16)


def _row_chunks(tm):
    return [slice(r, r + ROW_CHUNK) for r in range(0, tm, ROW_CHUNK)]


def _skewed(chunks, matmul_fn, epilogue_fn):
    pending = None
    for rows in chunks:
        acc = matmul_fn(rows)
        if pending is not None:
            epilogue_fn(*pending)
        pending = (rows, acc)
    epilogue_fn(*pending)


def _even_out_kernel(attn_ref, b_ref, c_ref, u_ref, cp_ref, up_ref, cn_ref, un_ref, cw_ref, wo_ref,
                     x_ref, gate_ref, gn_ref, shf_ref, scf_ref, x1_ref, h_ref, conv_scr, *, tiles_per_mod, halo):
    i = pl.program_id(0)
    tm = c_ref.shape[0]
    t = i % tiles_per_mod
    chunks = _row_chunks(tm)
    mix_attn = [_dot(attn_ref[rows, :], wo_ref[0:NA_WIDTH, :]) for rows in chunks]
    z = c_ref[...].astype(F32) * u_ref[...].astype(F32)
    z_before = cp_ref[halo - 1:halo, :].astype(F32) * up_ref[halo - 1:halo, :].astype(F32)
    z_after = cn_ref[0:1, :].astype(F32) * un_ref[0:1, :].astype(F32)
    z_before = jnp.where(t == 0, 0.0, z_before)
    z_after = jnp.where(t == tiles_per_mod - 1, 0.0, z_after)
    ridx = lax.broadcasted_iota(jnp.int32, (tm, 1), 0)
    z_prev = jnp.where(ridx == 0, z_before, pltpu.roll(z, 1, 0))
    z_next = jnp.where(ridx == tm - 1, z_after, pltpu.roll(z, tm - 1, 0))
    y = z_prev * cw_ref[0:1, :] + z * cw_ref[1:2, :] + z_next * cw_ref[2:3, :]
    conv_scr[...] = (b_ref[...].astype(F32) * y).astype(BF16)
    _skewed(
        chunks,
        lambda rows: mix_attn[chunks.index(rows)] + _dot(conv_scr[rows, :], wo_ref[NA_WIDTH:, :]),
        lambda rows, mix: _residual_and_ffn_norm(rows, i // tiles_per_mod, x_ref, mix, gate_ref, gn_ref,
                                                 shf_ref, scf_ref, x1_ref, h_ref))


def _even_out(attn, proj, conv_w, w_out, x2d, mods, norm_ffn_g, layer, seq, tm):
    m = x2d.shape[0]
    halo = 16
    hb = tm // halo
    n_halo_blocks = m // halo
    cb = NA_WIDTH * 3 // CONV_CH
    kern = functools.partial(_even_out_kernel, tiles_per_mod=seq // tm, halo=halo)
    main = lambda c: pl.BlockSpec((tm, CONV_CH), lambda i: (i, c))
    prev = lambda c: pl.BlockSpec((halo, CONV_CH), lambda i: (jnp.maximum(i * hb - 1, 0), c))
    nxt = lambda c: pl.BlockSpec((halo, CONV_CH), lambda i: (jnp.minimum((i + 1) * hb, n_halo_blocks - 1), c))
    return pl.pallas_call(
        kern,
        grid=(m // tm,),
        in_specs=[
            pl.BlockSpec((tm, NA_WIDTH), lambda i: (i, 0)),
            main(cb), main(cb + 1), main(cb + 2),
            prev(cb + 1), prev(cb + 2), nxt(cb + 1), nxt(cb + 2),
            _resident_spec(conv_w.shape),
            _resident_spec(w_out.shape),
            pl.BlockSpec((tm, D_MODEL), lambda i: (i, 0)),
            _mod_spec(layer, 2, 1),
            pl.BlockSpec((None, 1, D_MODEL), lambda i: (layer, 0, 0)),
            _mod_spec(layer, 3, 1),
            _mod_spec(layer, 4, 1),
        ],
        out_specs=[
            pl.BlockSpec((tm, D_MODEL), lambda i: (i, 0)),
            pl.BlockSpec((tm, D_MODEL), lambda i: (i, 0)),
        ],
        out_shape=[
            jax.ShapeDtypeStruct((m, D_MODEL), F32),
            jax.ShapeDtypeStruct((m, D_MODEL), BF16),
        ],
        scratch_shapes=[pltpu.VMEM((tm, CONV_CH), BF16)],
        compiler_params=_params(1),
        name="even_out",
    )(attn, proj, proj, proj, proj, proj, proj, proj, conv_w, w_out, x2d, mods, norm_ffn_g, mods, mods)


def _ffn_kernel(*refs, n_row_tiles, tiles_per_mod, modulated):
    if modulated:
        (h_ref, wg_ref, wu_ref, wd_ref, x_ref, gate_ref, gn_ref, sh_ref, sc_ref,
         x2_ref, y_ref, acc_ref) = refs
    else:
        h_ref, wg_ref, wu_ref, wd_ref, x_ref, gate_ref, gn_ref, y_ref, acc_ref = refs
    i = pl.program_id(0)
    f = pl.program_id(1)
    slot = i % 2
    tm = h_ref.shape[0]
    rs = tm // FFN_EPILOGUE_SLICES

    @pl.when((f == 0) & (i < n_row_tiles))
    def _():
        acc_ref[slot] = jnp.zeros((tm, D_MODEL), F32)

    def matmuls():
        rc = tm // FFN_ROW_CHUNKS
        ts = []
        for c in range(FFN_ROW_CHUNKS):
            h = h_ref[c * rc:(c + 1) * rc, :]
            ts.append((jax.nn.silu(_dot(h, wg_ref[...])) * _dot(h, wu_ref[...])).astype(BF16))
        for c in range(FFN_ROW_CHUNKS):
            acc_ref[slot, c * rc:(c + 1) * rc, :] += _dot(ts[c], wd_ref[...])

    def epilogue_slice():
        r0 = pl.multiple_of(jnp.minimum(f, FFN_EPILOGUE_SLICES - 1) * rs, rs)
        row = (i - 1) // tiles_per_mod
        x2 = x_ref[...] + gate_ref[pl.ds(row, 1), :] * acc_ref[1 - slot, pl.ds(r0, rs), :]
        if modulated:
            x2_ref[...] = x2
            y = _norm_mod(x2, gn_ref[...], sh_ref[pl.ds(row, 1), :], sc_ref[pl.ds(row, 1), :])
        else:
            y = x2 * lax.rsqrt(jnp.mean(x2 * x2, axis=-1, keepdims=True) + EPS) * gn_ref[...]
        y_ref[...] = y.astype(y_ref.dtype)

    @pl.when(i == 0)
    def _():
        matmuls()

    @pl.when((i > 0) & (i < n_row_tiles))
    def _():
        epilogue_slice()
        matmuls()

    @pl.when(i == n_row_tiles)
    def _():
        epilogue_slice()


def _ffn(h, w_gate, w_up, w_down, x2d, mods, layer, next_g, next_layer, seq, tm):
    m = x2d.shape[0]
    n_f, _, tf = w_gate.shape
    n_i = m // tm
    n_s = FFN_EPILOGUE_SLICES
    assert n_f >= n_s and (tm // n_s) % BF16_SUBLANES == 0
    modulated = next_layer is not None
    kern = functools.partial(_ffn_kernel, n_row_tiles=n_i, tiles_per_mod=seq // tm, modulated=modulated)
    w_tile = lambda i, f: jnp.where(i == n_i, n_f - 1, f)
    e_block = lambda i, f: jnp.where(i == 0, 0, (i - 1) * n_s + jnp.minimum(f, n_s - 1))
    slice_spec = pl.BlockSpec((tm // n_s, D_MODEL), lambda i, f: (e_block(i, f), 0))
    in_specs = [
        pl.BlockSpec((tm, D_MODEL), lambda i, f: (jnp.minimum(i, n_i - 1), 0)),
        pl.BlockSpec((None, D_MODEL, tf), lambda i, f: (w_tile(i, f), 0, 0)),
        pl.BlockSpec((None, D_MODEL, tf), lambda i, f: (w_tile(i, f), 0, 0)),
        pl.BlockSpec((tf, D_MODEL), lambda i, f: (w_tile(i, f), 0)),
        slice_spec,
        _mod_spec(layer, 5, 2),
    ]
    args = [h, w_gate, w_up, w_down, x2d, mods]
    if modulated:
        in_specs += [pl.BlockSpec((None, 1, D_MODEL), lambda i, f: (next_layer, 0, 0)),
                     _mod_spec(next_layer, 0, 2), _mod_spec(next_layer, 1, 2)]
        args += [next_g, mods, mods]
        out_specs = [slice_spec, slice_spec]
        out_shape = [jax.ShapeDtypeStruct((m, D_MODEL), F32), jax.ShapeDtypeStruct((m, D_MODEL), BF16)]
    else:
        in_specs += [pl.BlockSpec((1, D_MODEL), lambda i, f: (0, 0))]
        args += [next_g]
        out_specs = slice_spec
        out_shape = jax.ShapeDtypeStruct((m, D_MODEL), F32)
    return pl.pallas_call(
        kern,
        grid=(n_i + 1, n_f),
        in_specs=in_specs,
        out_specs=out_specs,
        out_shape=out_shape,
        scratch_shapes=[pltpu.VMEM((2, tm, D_MODEL), F32)],
        compiler_params=_params(2),
        name="ffn_final" if not modulated else "ffn",
    )(*args)


def _odd_in_kernel(h_ref, w_ref, lg_ref, lb_ref, u_ref, v_ref):
    def matmuls(rows):
        h = h_ref[rows, :]
        return _dot(h, w_ref[:, :D_MODEL]), _dot(h, w_ref[:, D_MODEL:])

    def epilogue(rows, acc):
        u_ref[rows, :] = jax.nn.gelu(acc[0], approximate=True).astype(BF16)
        z = jax.nn.gelu(acc[1], approximate=True)
        zc = z - jnp.mean(z, axis=-1, keepdims=True)
        var = jnp.mean(zc * zc, axis=-1, keepdims=True)
        v_ref[rows, :] = (zc * lax.rsqrt(var + EPS) * lg_ref[...] + lb_ref[...]).astype(BF16)

    _skewed(_row_chunks(h_ref.shape[0]), matmuls, epilogue)


def _odd_in(h, w_in, ln_g, ln_b, tm):
    m = h.shape[0]
    row_spec = pl.BlockSpec((tm, D_MODEL), lambda i: (i, 0))
    return pl.pallas_call(
        _odd_in_kernel,
        grid=(m // tm,),
        in_specs=[
            row_spec,
            _resident_spec(w_in.shape),
            pl.BlockSpec((1, D_MODEL), lambda i: (0, 0)),
            pl.BlockSpec((1, D_MODEL), lambda i: (0, 0)),
        ],
        out_specs=[row_spec, row_spec],
        out_shape=[jax.ShapeDtypeStruct((m, D_MODEL), BF16)] * 2,
        compiler_params=_params(1),
        name="odd_in",
    )(h, w_in, ln_g, ln_b)


def _odd_out_kernel(u_ref, v_ref, ws_ref, bs_ref, wo_ref, x_ref, gate_ref, gn_ref, shf_ref, scf_ref,
                    x1_ref, h_ref, t_scr, *, tiles_per_mod):
    i = pl.program_id(0)
    n_chunks = u_ref.shape[0] // CHUNK
    for g in range(SG_GROUPS):
        cols = slice(g * CHUNK, (g + 1) * CHUNK)
        vcat = jnp.concatenate([v_ref[n * CHUNK:(n + 1) * CHUNK, cols] for n in range(n_chunks)], axis=1)
        mixed = _dot(ws_ref[g], vcat)
        for n in range(n_chunks):
            rows = slice(n * CHUNK, (n + 1) * CHUNK)
            gated = u_ref[rows, cols].astype(F32) * (mixed[:, n * CHUNK:(n + 1) * CHUNK] + bs_ref[g])
            t_scr[rows, cols] = gated.astype(BF16)
    _skewed(
        _row_chunks(u_ref.shape[0]),
        lambda rows: _dot(t_scr[rows, :], wo_ref[...]),
        lambda rows, mix: _residual_and_ffn_norm(rows, i // tiles_per_mod, x_ref, mix, gate_ref, gn_ref,
                                                 shf_ref, scf_ref, x1_ref, h_ref))


def _odd_out(u, v, w_s, b_s, w_out, x2d, mods, norm_ffn_g, layer, seq, tm):
    m = x2d.shape[0]
    kern = functools.partial(_odd_out_kernel, tiles_per_mod=seq // tm)
    return pl.pallas_call(
        kern,
        grid=(m // tm,),
        in_specs=[
            pl.BlockSpec((tm, D_MODEL), lambda i: (i, 0)),
            pl.BlockSpec((tm, D_MODEL), lambda i: (i, 0)),
            _resident_spec(w_s.shape),
            _resident_spec(b_s.shape),
            _resident_spec(w_out.shape),
            pl.BlockSpec((tm, D_MODEL), lambda i: (i, 0)),
            _mod_spec(layer, 2, 1),
            pl.BlockSpec((None, 1, D_MODEL), lambda i: (layer, 0, 0)),
            _mod_spec(layer, 3, 1),
            _mod_spec(layer, 4, 1),
        ],
        out_specs=[
            pl.BlockSpec((tm, D_MODEL), lambda i: (i, 0)),
            pl.BlockSpec((tm, D_MODEL), lambda i: (i, 0)),
        ],
        out_shape=[
            jax.ShapeDtypeStruct((m, D_MODEL), F32),
            jax.ShapeDtypeStruct((m, D_MODEL), BF16),
        ],
        scratch_shapes=[pltpu.VMEM((tm, D_MODEL), BF16)],
        compiler_params=_params(1),
        name="odd_out",
    )(u, v, w_s, b_s, w_out, x2d, mods, norm_ffn_g, mods, mods)


def kernel(x, c, ctx, c_ctx, ada_w, ada_b, norm_mix_g, norm_ffn_g, ffn_w_gate, ffn_w_up, ffn_w_down,
           even_w_in, even_w_out, rpb, conv_w, odd_w_in, odd_w_out, sg_ln_g, sg_ln_b, sg_w, sg_b,
           final_norm_g):
    batch, seq, d = x.shape
    ctx_len = ctx.shape[1]
    depth = ada_w.shape[0]
    assert d == D_MODEL and depth == 2 and batch < MOD_ROWS and seq % (GRID_W * WIN_ROWS) == 0

    bf = lambda w: w.astype(BF16)
    x2d = x.reshape(batch * seq, d)
    ctx2d = ctx.reshape(batch * ctx_len, d)
    g_mix = norm_mix_g.reshape(depth, 1, d)
    g_ffn = norm_ffn_g.reshape(depth, 1, d)

    cond = jnp.concatenate([c, c_ctx[None], jnp.zeros((MOD_ROWS - batch - 1, d), F32)], axis=0)
    mods = _ada_mods(cond, ada_w, ada_b)

    ffn_casts = lambda layer: [(ffn_w_gate, layer, FFN_TILE), (ffn_w_up, layer, FFN_TILE),
                               (ffn_w_down, layer, None)]
    w_in0 = bf(even_w_in[0])
    proj, w_out0, wg0, wu0, wd0 = _norm_matmul(
        x2d, g_mix, mods, 0, w_in0, col0=0, n_cols=w_in0.shape[1], rows_per_mod=seq, mod_row0=0,
        tm=1024, tn=1024, name="even_in", cast_weights=[(even_w_out, 0, None)] + ffn_casts(0))
    ctx_kv, = _norm_matmul(ctx2d, g_mix, mods, 0, w_in0, col0=NA_WIDTH, n_cols=2 * NA_WIDTH,
                           rows_per_mod=batch * ctx_len, mod_row0=batch, tm=1024, tn=1024, name="ctx_kv")
    bias, cap = _bias_tables(rpb[0])
    attn, w_in1, w_out1, wg1, wu1, wd1 = _attention(
        proj, ctx_kv, bias, cap, batch, seq, ctx_len,
        cast_weights=[(odd_w_in, 0, None), (odd_w_out, 0, None)] + ffn_casts(1))
    x1, h = _even_out(attn, proj, conv_w[0], w_out0, x2d, mods, g_ffn, 0, seq, tm=512)
    x2, h = _ffn(h, wg0, wu0, wd0, x1, mods, 0, g_mix, 1, seq, tm=1024)

    u, v = _odd_in(h, w_in1, sg_ln_g[0][None], sg_ln_b[0][None], tm=512)
    b_s = jnp.broadcast_to(sg_b[0][:, :, None], (SG_GROUPS, CHUNK, CHUNK))
    x1, h = _odd_out(u, v, bf(sg_w[0]), b_s, w_out1, x2, mods, g_ffn, 1, seq, tm=512)
    out = _ffn(h, wg1, wu1, wd1, x1, mods, 1, final_norm_g[None], None, seq, tm=1024)
    return out.reshape(batch, seq, d)
```

```python
import functools

import jax
import jax.numpy as jnp
from jax import lax
from jax.experimental import pallas as pl
from jax.experimental.pallas import tpu as pltpu

D_MODEL = 2048
GRID_W = 64
NA_HEADS = 8
HEAD_DIM = 128
NA_WIDTH = NA_HEADS * HEAD_DIM
CONV_CH = D_MODEL - NA_WIDTH
WIN_ROWS = 8
WIN_COLS = 16
CHUNK = 128
SG_GROUPS = 16
EPS = 1e-6
NEG_INF = -1e30
UNMASKED_CAP = 3e38
MOD_ROWS = 16
NORM_SLICES = 4
FFN_TILE = 512
FFN_ROW_CHUNKS = 2
FFN_EPILOGUE_SLICES = 8
ROW_CHUNK = 256

F32 = jnp.float32
BF16 = jnp.bfloat16
BF16_SUBLANES = 16

VMEM_LIMIT_BYTES = 56 * 1024 * 1024


def _params(n_axes):
    return pltpu.CompilerParams(dimension_semantics=("arbitrary",) * n_axes,
                                vmem_limit_bytes=VMEM_LIMIT_BYTES)


def _resident_spec(shape):
    zeros = (0,) * len(shape)
    return pl.BlockSpec(shape, lambda *_: zeros, pipeline_mode=pl.Buffered(1))


def _dot(a, b):
    return jnp.dot(a, b, preferred_element_type=F32)


def _norm_mod(x, g, shift, scale):
    y = x * lax.rsqrt(jnp.mean(x * x, axis=-1, keepdims=True) + EPS)
    return (y * g) * (1.0 + scale) + shift


def _linear_step(ids, grid):
    step = ids[0]
    for axis in range(1, len(grid)):
        step = step * grid[axis] + ids[axis]
    return step


class _Cast:
    def __init__(self, stacked_w, layer, grid, col_tile=None):
        _, rows, cols = stacked_w.shape
        n_steps = 1
        for g in grid:
            n_steps *= g
        self.w = stacked_w
        self.grid = grid
        self.col_tile = col_tile
        self.n_blocks = nb = max(n for n in range(1, n_steps + 1)
                                 if rows % n == 0 and (rows // n) % BF16_SUBLANES == 0)
        rb = rows // nb
        block = lambda *ids: jnp.minimum(_linear_step(ids, grid), nb - 1)
        self.in_spec = pl.BlockSpec((None, rb, cols), lambda *ids: (layer, block(*ids), 0))
        if col_tile is None:
            self.out_shape = jax.ShapeDtypeStruct((rows, cols), BF16)
            self.out_spec = pl.BlockSpec((rb, cols), lambda *ids: (block(*ids), 0))
        else:
            nt = cols // col_tile
            self.out_shape = jax.ShapeDtypeStruct((nt, rows, col_tile), BF16)
            self.out_spec = pl.BlockSpec((nt, rb, col_tile), lambda *ids: (0, block(*ids), 0))

    def run(self, w_ref, o_ref):
        step = _linear_step([pl.program_id(a) for a in range(len(self.grid))], self.grid)

        @pl.when(step < self.n_blocks)
        def _():
            if self.col_tile is None:
                o_ref[...] = w_ref[...].astype(BF16)
            else:
                for t in range(o_ref.shape[0]):
                    o_ref[t] = w_ref[:, t * self.col_tile:(t + 1) * self.col_tile].astype(BF16)


def _with_casts(body, n_in, n_out, casts):
    n_c = len(casts)

    def kern(*refs):
        ins, rest = refs[:n_in], refs[n_in:]
        cast_ins, rest = rest[:n_c], rest[n_c:]
        outs, rest = rest[:n_out], rest[n_out:]
        cast_outs, scratch = rest[:n_c], rest[n_c:]
        body(*ins, *outs, *scratch)
        for cast, w_ref, o_ref in zip(casts, cast_ins, cast_outs):
            cast.run(w_ref, o_ref)

    return kern


def _ada_kernel(cond_ref, w_ref, b_ref, o_ref):
    a = jax.nn.silu(cond_ref[...]).astype(BF16)
    o_ref[...] = _dot(a, w_ref[...].astype(BF16)) + b_ref[...]


def _ada_mods(cond, ada_w, ada_b):
    depth, _, n = ada_w.shape
    tn = 1024
    return pl.pallas_call(
        _ada_kernel,
        grid=(depth, n // tn),
        in_specs=[
            pl.BlockSpec((MOD_ROWS, D_MODEL), lambda l, j: (0, 0)),
            pl.BlockSpec((None, D_MODEL, tn), lambda l, j: (l, 0, j)),
            pl.BlockSpec((None, 1, tn), lambda l, j: (l, 0, j)),
        ],
        out_specs=pl.BlockSpec((None, MOD_ROWS, tn), lambda l, j: (l, 0, j)),
        out_shape=jax.ShapeDtypeStruct((depth, MOD_ROWS, n), F32),
        compiler_params=_params(2),
        name="ada_mods",
    )(cond, ada_w, ada_b.reshape(depth, 1, n))


def _mod_spec(layer, chunk, n_grid_axes):
    if n_grid_axes == 1:
        return pl.BlockSpec((None, MOD_ROWS, D_MODEL), lambda i: (layer, 0, chunk))
    return pl.BlockSpec((None, MOD_ROWS, D_MODEL), lambda i, j: (layer, 0, chunk))


def _norm_matmul_kernel(x_ref, g_ref, sh_ref, sc_ref, w_ref, o_ref, h_scr, *,
                        n_row_tiles, n_slices, tiles_per_mod, mod_row0):
    i = pl.program_id(0)
    j = pl.program_id(1)
    rs = x_ref.shape[0]

    def norm_slice():
        r0 = pl.multiple_of(jnp.minimum(j, n_slices - 1) * rs, rs)
        row = mod_row0 + i // tiles_per_mod
        h = _norm_mod(x_ref[...], g_ref[...], sh_ref[pl.ds(row, 1), :], sc_ref[pl.ds(row, 1), :])
        h_scr[i % 2, pl.ds(r0, rs), :] = h.astype(BF16)

    def matmul():
        o_ref[...] = _dot(h_scr[(i - 1) % 2], w_ref[...]).astype(o_ref.dtype)

    @pl.when(i == 0)
    def _():
        norm_slice()

    @pl.when((i > 0) & (i < n_row_tiles))
    def _():
        matmul()
        norm_slice()

    @pl.when(i == n_row_tiles)
    def _():
        matmul()


def _norm_matmul(x2d, g, mods, layer, w, *, col0, n_cols, rows_per_mod, mod_row0, tm, tn, name,
                 cast_weights=()):
    m = x2d.shape[0]
    n_i, n_j = m // tm, n_cols // tn
    n_s = min(n_j, NORM_SLICES)
    grid = (n_i + 1, n_j)
    body = functools.partial(_norm_matmul_kernel, n_row_tiles=n_i, n_slices=n_s,
                             tiles_per_mod=max(rows_per_mod // tm, 1), mod_row0=mod_row0)
    casts = [_Cast(cw, layer_, grid, col_tile) for cw, layer_, col_tile in cast_weights]
    cb0 = col0 // tn
    x_block = lambda i, j: jnp.minimum(i, n_i - 1) * n_s + jnp.minimum(j, n_s - 1)
    w_block = lambda i, j: cb0 + jnp.where(i == 0, 0, j)
    return pl.pallas_call(
        _with_casts(body, 5, 1, casts),
        grid=grid,
        in_specs=[
            pl.BlockSpec((tm // n_s, D_MODEL), lambda i, j: (x_block(i, j), 0)),
            pl.BlockSpec((None, 1, D_MODEL), lambda i, j: (layer, 0, 0)),
            _mod_spec(layer, 0, 2),
            _mod_spec(layer, 1, 2),
            pl.BlockSpec((D_MODEL, tn), lambda i, j: (0, w_block(i, j))),
        ] + [c.in_spec for c in casts],
        out_specs=[pl.BlockSpec((tm, tn), lambda i, j: (jnp.maximum(i - 1, 0), jnp.where(i == 0, 0, j)))]
        + [c.out_spec for c in casts],
        out_shape=[jax.ShapeDtypeStruct((m, n_cols), BF16)] + [c.out_shape for c in casts],
        scratch_shapes=[pltpu.VMEM((2, tm, D_MODEL), BF16)],
        compiler_params=_params(2),
        name=name,
    )(x2d, g, mods, mods, w, *[c.w for c in casts])


def _bias_kernel(rpb_ref, bias_ref, cap_ref):
    h = pl.program_id(0)
    n_dc = 2 * WIN_COLS - 1
    n_dr = 2 * WIN_ROWS - 1
    q = lax.broadcasted_iota(jnp.int32, (GRID_W, 2 * GRID_W), 0)
    lane = lax.broadcasted_iota(jnp.int32, (GRID_W, 2 * GRID_W), 1)
    second = lane >= GRID_W
    k = jnp.where(second, lane - GRID_W, lane)
    dc = jnp.clip(k - q, -(WIN_COLS - 1), WIN_COLS - 1) + (WIN_COLS - 1)
    pairs = []
    for m in range(n_dr - 1):
        acc = jnp.zeros((GRID_W, 2 * GRID_W), F32)
        for d in range(n_dc):
            val = jnp.where(second, rpb_ref[h, (m + 1) * n_dc + d], rpb_ref[h, m * n_dc + d])
            acc = jnp.where(dc == d, val, acc)
        pairs.append(acc)
    for o in range(WIN_ROWS):
        for jj in range(WIN_ROWS // 2):
            bias_ref[o, :, jj * 2 * GRID_W:(jj + 1) * 2 * GRID_W] = pairs[o + 2 * jj]
    col_start = jnp.clip(q - WIN_COLS // 2, 0, GRID_W - WIN_COLS)
    cap = jnp.where((k >= col_start) & (k < col_start + WIN_COLS), UNMASKED_CAP, NEG_INF).astype(F32)
    for jj in range(WIN_ROWS // 2):
        cap_ref[:, jj * 2 * GRID_W:(jj + 1) * 2 * GRID_W] = cap


def _bias_tables(rpb):
    h = rpb.shape[0]
    kw = WIN_ROWS * GRID_W
    return pl.pallas_call(
        _bias_kernel,
        grid=(h,),
        in_specs=[pl.BlockSpec(memory_space=pltpu.SMEM)],
        out_specs=[
            pl.BlockSpec((None, WIN_ROWS, GRID_W, kw), lambda i: (i, 0, 0, 0)),
            pl.BlockSpec((GRID_W, kw), lambda i: (0, 0)),
        ],
        out_shape=[
            jax.ShapeDtypeStruct((h, WIN_ROWS, GRID_W, kw), F32),
            jax.ShapeDtypeStruct((GRID_W, kw), F32),
        ],
        compiler_params=_params(1),
        name="bias_tables",
    )(rpb.reshape(h, -1))


def _attn_kernel(q_ref, k_ref, v_ref, kc_ref, vc_ref, bias_ref, cap_ref, o_ref,
                 s_scr, e_scr, vx_scr, vcx_scr, acc_scr, *, rows):
    scale = HEAD_DIM ** -0.5
    kw = WIN_ROWS * GRID_W
    ctx_len = kc_ref.shape[0]
    nt = (((1,), (1,)), ((), ()))

    vx_scr[:, :HEAD_DIM] = v_ref[...]
    vx_scr[:, HEAD_DIM:] = jnp.ones((v_ref.shape[0], HEAD_DIM), BF16)
    vcx_scr[:, :HEAD_DIM] = vc_ref[...]
    vcx_scr[:, HEAD_DIM:] = jnp.ones((ctx_len, HEAD_DIM), BF16)

    s_scr[:, kw:] = lax.dot_general(q_ref[...], kc_ref[...], nt, preferred_element_type=F32) * scale

    def scores(r, carry):
        row_start = jnp.clip(r - WIN_ROWS // 2, 0, rows - WIN_ROWS)
        o = row_start - r + (WIN_ROWS - 1)
        q0 = pl.multiple_of(r * GRID_W, GRID_W)
        k0 = pl.multiple_of(row_start * GRID_W, GRID_W)
        s = lax.dot_general(q_ref[pl.ds(q0, GRID_W), :], k_ref[pl.ds(k0, kw), :], nt,
                            preferred_element_type=F32)
        s_scr[pl.ds(q0, GRID_W), :kw] = jnp.minimum(s * scale + bias_ref[o], cap_ref[...])
        return carry

    lax.fori_loop(0, rows, scores, 0, unroll=16)

    def expo(r, carry):
        q0 = pl.multiple_of(r * GRID_W, GRID_W)
        s = s_scr[pl.ds(q0, GRID_W), :]
        e_scr[pl.ds(q0, GRID_W), :] = jnp.exp(s - jnp.max(s, axis=-1, keepdims=True)).astype(BF16)
        return carry

    lax.fori_loop(0, rows, expo, 0, unroll=16)

    acc_scr[...] = _dot(e_scr[:, kw:], vcx_scr[...])

    def values(r, carry):
        row_start = jnp.clip(r - WIN_ROWS // 2, 0, rows - WIN_ROWS)
        q0 = pl.multiple_of(r * GRID_W, GRID_W)
        k0 = pl.multiple_of(row_start * GRID_W, GRID_W)
        t = acc_scr[pl.ds(q0, GRID_W), :] + _dot(e_scr[pl.ds(q0, GRID_W), :kw], vx_scr[pl.ds(k0, kw), :])
        o_ref[pl.ds(q0, GRID_W), :] = (t[:, :HEAD_DIM] / t[:, HEAD_DIM:]).astype(o_ref.dtype)
        return carry

    lax.fori_loop(0, rows, values, 0, unroll=16)


def _attention(qkv, ctx_kv, bias, cap, batch, seq, ctx_len, cast_weights=()):
    rows = seq // GRID_W
    kw = WIN_ROWS * GRID_W
    grid = (NA_HEADS, batch)
    casts = [_Cast(cw, layer_, grid, col_tile) for cw, layer_, col_tile in cast_weights]
    return pl.pallas_call(
        _with_casts(functools.partial(_attn_kernel, rows=rows), 7, 1, casts),
        grid=grid,
        in_specs=[
            pl.BlockSpec((seq, HEAD_DIM), lambda h, b: (b, h)),
            pl.BlockSpec((seq, HEAD_DIM), lambda h, b: (b, NA_HEADS + h)),
            pl.BlockSpec((seq, HEAD_DIM), lambda h, b: (b, 2 * NA_HEADS + h)),
            pl.BlockSpec((ctx_len, HEAD_DIM), lambda h, b: (b, h)),
            pl.BlockSpec((ctx_len, HEAD_DIM), lambda h, b: (b, NA_HEADS + h)),
            pl.BlockSpec((None, WIN_ROWS, GRID_W, kw), lambda h, b: (h, 0, 0, 0)),
            pl.BlockSpec((GRID_W, kw), lambda h, b: (0, 0)),
        ] + [c.in_spec for c in casts],
        out_specs=[pl.BlockSpec((seq, HEAD_DIM), lambda h, b: (b, h))] + [c.out_spec for c in casts],
        out_shape=[jax.ShapeDtypeStruct((batch * seq, NA_WIDTH), BF16)] + [c.out_shape for c in casts],
        scratch_shapes=[
            pltpu.VMEM((seq, kw + ctx_len), F32),
            pltpu.VMEM((seq, kw + ctx_len), BF16),
            pltpu.VMEM((seq, 2 * HEAD_DIM), BF16),
            pltpu.VMEM((ctx_len, 2 * HEAD_DIM), BF16),
            pltpu.VMEM((seq, 2 * HEAD_DIM), F32),
        ],
        compiler_params=_params(2),
        name="nbr_attention",
    )(qkv, qkv, qkv, ctx_kv, ctx_kv, bias, cap, *[c.w for c in casts])


def _residual_and_ffn_norm(rows, mod_row, x_ref, mix, gate_ref, gn_ref, shf_ref, scf_ref, x1_ref, h_ref):
    x1 = x_ref[rows, :] + gate_ref[pl.ds(mod_row, 1), :] * mix
    x1_ref[rows, :] = x1
    h = _norm_mod(x1, gn_ref[...], shf_ref[pl.ds(mod_row, 1), :], scf_ref[pl.ds(mod_row, 1), :])
    h_ref[rows, :] = h.astype(BF16)


def _row_chunks(tm):
    return [slice(r, r + ROW_CHUNK) for r in range(0, tm, ROW_CHUNK)]


def _skewed(chunks, matmul_fn, epilogue_fn):
    pending = None
    for rows in chunks:
        acc = matmul_fn(rows)
        if pending is not None:
            epilogue_fn(*pending)
        pending = (rows, acc)
    epilogue_fn(*pending)


def _even_out_kernel(attn_ref, b_ref, c_ref, u_ref, cp_ref, up_ref, cn_ref, un_ref, cw_ref, wo_ref,
                     x_ref, gate_ref, gn_ref, shf_ref, scf_ref, x1_ref, h_ref, conv_scr, *, tiles_per_mod, halo):
    i = pl.program_id(0)
    tm = c_ref.shape[0]
    t = i % tiles_per_mod
    chunks = _row_chunks(tm)
    mix_attn = [_dot(attn_ref[rows, :], wo_ref[0:NA_WIDTH, :]) for rows in chunks]
    z = c_ref[...].astype(F32) * u_ref[...].astype(F32)
    z_before = cp_ref[halo - 1:halo, :].astype(F32) * up_ref[halo - 1:halo, :].astype(F32)
    z_after = cn_ref[0:1, :].astype(F32) * un_ref[0:1, :].astype(F32)
    z_before = jnp.where(t == 0, 0.0, z_before)
    z_after = jnp.where(t == tiles_per_mod - 1, 0.0, z_after)
    ridx = lax.broadcasted_iota(jnp.int32, (tm, 1), 0)
    z_prev = jnp.where(ridx == 0, z_before, pltpu.roll(z, 1, 0))
    z_next = jnp.where(ridx == tm - 1, z_after, pltpu.roll(z, tm - 1, 0))
    y = z_prev * cw_ref[0:1, :] + z * cw_ref[1:2, :] + z_next * cw_ref[2:3, :]
    conv_scr[...] = (b_ref[...].astype(F32) * y).astype(BF16)
    _skewed(
        chunks,
        lambda rows: mix_attn[chunks.index(rows)] + _dot(conv_scr[rows, :], wo_ref[NA_WIDTH:, :]),
        lambda rows, mix: _residual_and_ffn_norm(rows, i // tiles_per_mod, x_ref, mix, gate_ref, gn_ref,
                                                 shf_ref, scf_ref, x1_ref, h_ref))


def _even_out(attn, proj, conv_w, w_out, x2d, mods, norm_ffn_g, layer, seq, tm):
    m = x2d.shape[0]
    halo = 16
    hb = tm // halo
    n_halo_blocks = m // halo
    cb = NA_WIDTH * 3 // CONV_CH
    kern = functools.partial(_even_out_kernel, tiles_per_mod=seq // tm, halo=halo)
    main = lambda c: pl.BlockSpec((tm, CONV_CH), lambda i: (i, c))
    prev = lambda c: pl.BlockSpec((halo, CONV_CH), lambda i: (jnp.maximum(i * hb - 1, 0), c))
    nxt = lambda c: pl.BlockSpec((halo, CONV_CH), lambda i: (jnp.minimum((i + 1) * hb, n_halo_blocks - 1), c))
    return pl.pallas_call(
        kern,
        grid=(m // tm,),
        in_specs=[
            pl.BlockSpec((tm, NA_WIDTH), lambda i: (i, 0)),
            main(cb), main(cb + 1), main(cb + 2),
            prev(cb + 1), prev(cb + 2), nxt(cb + 1), nxt(cb + 2),
            _resident_spec(conv_w.shape),
            _resident_spec(w_out.shape),
            pl.BlockSpec((tm, D_MODEL), lambda i: (i, 0)),
            _mod_spec(layer, 2, 1),
            pl.BlockSpec((None, 1, D_MODEL), lambda i: (layer, 0, 0)),
            _mod_spec(layer, 3, 1),
            _mod_spec(layer, 4, 1),
        ],
        out_specs=[
            pl.BlockSpec((tm, D_MODEL), lambda i: (i, 0)),
            pl.BlockSpec((tm, D_MODEL), lambda i: (i, 0)),
        ],
        out_shape=[
            jax.ShapeDtypeStruct((m, D_MODEL), F32),
            jax.ShapeDtypeStruct((m, D_MODEL), BF16),
        ],
        scratch_shapes=[pltpu.VMEM((tm, CONV_CH), BF16)],
        compiler_params=_params(1),
        name="even_out",
    )(attn, proj, proj, proj, proj, proj, proj, proj, conv_w, w_out, x2d, mods, norm_ffn_g, mods, mods)


def _ffn_kernel(*refs, n_row_tiles, tiles_per_mod, modulated):
    if modulated:
        (h_ref, wg_ref, wu_ref, wd_ref, x_ref, gate_ref, gn_ref, sh_ref, sc_ref,
         x2_ref, y_ref, acc_ref) = refs
    else:
        h_ref, wg_ref, wu_ref, wd_ref, x_ref, gate_ref, gn_ref, y_ref, acc_ref = refs
    i = pl.program_id(0)
    f = pl.program_id(1)
    slot = i % 2
    tm = h_ref.shape[0]
    rs = tm // FFN_EPILOGUE_SLICES

    @pl.when((f == 0) & (i < n_row_tiles))
    def _():
        acc_ref[slot] = jnp.zeros((tm, D_MODEL), F32)

    def matmuls():
        rc = tm // FFN_ROW_CHUNKS
        ts = []
        for c in range(FFN_ROW_CHUNKS):
            h = h_ref[c * rc:(c + 1) * rc, :]
            ts.append((jax.nn.silu(_dot(h, wg_ref[...])) * _dot(h, wu_ref[...])).astype(BF16))
        for c in range(FFN_ROW_CHUNKS):
            acc_ref[slot, c * rc:(c + 1) * rc, :] += _dot(ts[c], wd_ref[...])

    def epilogue_slice():
        r0 = pl.multiple_of(jnp.minimum(f, FFN_EPILOGUE_SLICES - 1) * rs, rs)
        row = (i - 1) // tiles_per_mod
        x2 = x_ref[...] + gate_ref[pl.ds(row, 1), :] * acc_ref[1 - slot, pl.ds(r0, rs), :]
        if modulated:
            x2_ref[...] = x2
            y = _norm_mod(x2, gn_ref[...], sh_ref[pl.ds(row, 1), :], sc_ref[pl.ds(row, 1), :])
        else:
            y = x2 * lax.rsqrt(jnp.mean(x2 * x2, axis=-1, keepdims=True) + EPS) * gn_ref[...]
        y_ref[...] = y.astype(y_ref.dtype)

    @pl.when(i == 0)
    def _():
        matmuls()

    @pl.when((i > 0) & (i < n_row_tiles))
    def _():
        epilogue_slice()
        matmuls()

    @pl.when(i == n_row_tiles)
    def _():
        epilogue_slice()


def _ffn(h, w_gate, w_up, w_down, x2d, mods, layer, next_g, next_layer, seq, tm):
    m = x2d.shape[0]
    n_f, _, tf = w_gate.shape
    n_i = m // tm
    n_s = FFN_EPILOGUE_SLICES
    assert n_f >= n_s and (tm // n_s) % BF16_SUBLANES == 0
    modulated = next_layer is not None
    kern = functools.partial(_ffn_kernel, n_row_tiles=n_i, tiles_per_mod=seq // tm, modulated=modulated)
    w_tile = lambda i, f: jnp.where(i == n_i, n_f - 1, f)
    e_block = lambda i, f: jnp.where(i == 0, 0, (i - 1) * n_s + jnp.minimum(f, n_s - 1))
    slice_spec = pl.BlockSpec((tm // n_s, D_MODEL), lambda i, f: (e_block(i, f), 0))
    in_specs = [
        pl.BlockSpec((tm, D_MODEL), lambda i, f: (jnp.minimum(i, n_i - 1), 0)),
        pl.BlockSpec((None, D_MODEL, tf), lambda i, f: (w_tile(i, f), 0, 0)),
        pl.BlockSpec((None, D_MODEL, tf), lambda i, f: (w_tile(i, f), 0, 0)),
        pl.BlockSpec((tf, D_MODEL), lambda i, f: (w_tile(i, f), 0)),
        slice_spec,
        _mod_spec(layer, 5, 2),
    ]
    args = [h, w_gate, w_up, w_down, x2d, mods]
    if modulated:
        in_specs += [pl.BlockSpec((None, 1, D_MODEL), lambda i, f: (next_layer, 0, 0)),
                     _mod_spec(next_layer, 0, 2), _mod_spec(next_layer, 1, 2)]
        args += [next_g, mods, mods]
        out_specs = [slice_spec, slice_spec]
        out_shape = [jax.ShapeDtypeStruct((m, D_MODEL), F32), jax.ShapeDtypeStruct((m, D_MODEL), BF16)]
    else:
        in_specs += [pl.BlockSpec((1, D_MODEL), lambda i, f: (0, 0))]
        args += [next_g]
        out_specs = slice_spec
        out_shape = jax.ShapeDtypeStruct((m, D_MODEL), F32)
    return pl.pallas_call(
        kern,
        grid=(n_i + 1, n_f),
        in_specs=in_specs,
        out_specs=out_specs,
        out_shape=out_shape,
        scratch_shapes=[pltpu.VMEM((2, tm, D_MODEL), F32)],
        compiler_params=_params(2),
        name="ffn_final" if not modulated else "ffn",
    )(*args)


def _odd_in_kernel(h_ref, w_ref, lg_ref, lb_ref, u_ref, v_ref):
    def matmuls(rows):
        h = h_ref[rows, :]
        return _dot(h, w_ref[:, :D_MODEL]), _dot(h, w_ref[:, D_MODEL:])

    def epilogue(rows, acc):
        u_ref[rows, :] = jax.nn.gelu(acc[0], approximate=True).astype(BF16)
        z = jax.nn.gelu(acc[1], approximate=True)
        zc = z - jnp.mean(z, axis=-1, keepdims=True)
        var = jnp.mean(zc * zc, axis=-1, keepdims=True)
        v_ref[rows, :] = (zc * lax.rsqrt(var + EPS) * lg_ref[...] + lb_ref[...]).astype(BF16)

    _skewed(_row_chunks(h_ref.shape[0]), matmuls, epilogue)


def _odd_in(h, w_in, ln_g, ln_b, tm):
    m = h.shape[0]
    row_spec = pl.BlockSpec((tm, D_MODEL), lambda i: (i, 0))
    return pl.pallas_call(
        _odd_in_kernel,
        grid=(m // tm,),
        in_specs=[
            row_spec,
            _resident_spec(w_in.shape),
            pl.BlockSpec((1, D_MODEL), lambda i: (0, 0)),
            pl.BlockSpec((1, D_MODEL), lambda i: (0, 0)),
        ],
        out_specs=[row_spec, row_spec],
        out_shape=[jax.ShapeDtypeStruct((m, D_MODEL), BF16)] * 2,
        compiler_params=_params(1),
        name="odd_in",
    )(h, w_in, ln_g, ln_b)


def _odd_out_kernel(u_ref, v_ref, ws_ref, bs_ref, wo_ref, x_ref, gate_ref, gn_ref, shf_ref, scf_ref,
                    x1_ref, h_ref, t_scr, *, tiles_per_mod):
    i = pl.program_id(0)
    n_chunks = u_ref.shape[0] // CHUNK
    for g in range(SG_GROUPS):
        cols = slice(g * CHUNK, (g + 1) * CHUNK)
        vcat = jnp.concatenate([v_ref[n * CHUNK:(n + 1) * CHUNK, cols] for n in range(n_chunks)], axis=1)
        mixed = _dot(ws_ref[g], vcat)
        for n in range(n_chunks):
            rows = slice(n * CHUNK, (n + 1) * CHUNK)
            gated = u_ref[rows, cols].astype(F32) * (mixed[:, n * CHUNK:(n + 1) * CHUNK] + bs_ref[g])
            t_scr[rows, cols] = gated.astype(BF16)
    _skewed(
        _row_chunks(u_ref.shape[0]),
        lambda rows: _dot(t_scr[rows, :], wo_ref[...]),
        lambda rows, mix: _residual_and_ffn_norm(rows, i // tiles_per_mod, x_ref, mix, gate_ref, gn_ref,
                                                 shf_ref, scf_ref, x1_ref, h_ref))


def _odd_out(u, v, w_s, b_s, w_out, x2d, mods, norm_ffn_g, layer, seq, tm):
    m = x2d.shape[0]
    kern = functools.partial(_odd_out_kernel, tiles_per_mod=seq // tm)
    return pl.pallas_call(
        kern,
        grid=(m // tm,),
        in_specs=[
            pl.BlockSpec((tm, D_MODEL), lambda i: (i, 0)),
            pl.BlockSpec((tm, D_MODEL), lambda i: (i, 0)),
            _resident_spec(w_s.shape),
            _resident_spec(b_s.shape),
            _resident_spec(w_out.shape),
            pl.BlockSpec((tm, D_MODEL), lambda i: (i, 0)),
            _mod_spec(layer, 2, 1),
            pl.BlockSpec((None, 1, D_MODEL), lambda i: (layer, 0, 0)),
            _mod_spec(layer, 3, 1),
            _mod_spec(layer, 4, 1),
        ],
        out_specs=[
            pl.BlockSpec((tm, D_MODEL), lambda i: (i, 0)),
            pl.BlockSpec((tm, D_MODEL), lambda i: (i, 0)),
        ],
        out_shape=[
            jax.ShapeDtypeStruct((m, D_MODEL), F32),
            jax.ShapeDtypeStruct((m, D_MODEL), BF16),
        ],
        scratch_shapes=[pltpu.VMEM((tm, D_MODEL), BF16)],
        compiler_params=_params(1),
        name="odd_out",
    )(u, v, w_s, b_s, w_out, x2d, mods, norm_ffn_g, mods, mods)


def kernel(x, c, ctx, c_ctx, ada_w, ada_b, norm_mix_g, norm_ffn_g, ffn_w_gate, ffn_w_up, ffn_w_down,
           even_w_in, even_w_out, rpb, conv_w, odd_w_in, odd_w_out, sg_ln_g, sg_ln_b, sg_w, sg_b,
           final_norm_g):
    batch, seq, d = x.shape
    ctx_len = ctx.shape[1]
    depth = ada_w.shape[0]
    assert d == D_MODEL and depth == 2 and batch < MOD_ROWS and seq % (GRID_W * WIN_ROWS) == 0

    bf = lambda w: w.astype(BF16)
    x2d = x.reshape(batch * seq, d)
    ctx2d = ctx.reshape(batch * ctx_len, d)
    g_mix = norm_mix_g.reshape(depth, 1, d)
    g_ffn = norm_ffn_g.reshape(depth, 1, d)

    cond = jnp.concatenate([c, c_ctx[None], jnp.zeros((MOD_ROWS - batch - 1, d), F32)], axis=0)
    mods = _ada_mods(cond, ada_w, ada_b)

    ffn_casts = lambda layer: [(ffn_w_gate, layer, FFN_TILE), (ffn_w_up, layer, FFN_TILE),
                               (ffn_w_down, layer, None)]
    w_in0 = bf(even_w_in[0])
    proj, w_out0, wg0, wu0, wd0 = _norm_matmul(
        x2d, g_mix, mods, 0, w_in0, col0=0, n_cols=w_in0.shape[1], rows_per_mod=seq, mod_row0=0,
        tm=1024, tn=1536, name="even_in", cast_weights=[(even_w_out, 0, None)] + ffn_casts(0))
    ctx_kv, = _norm_matmul(ctx2d, g_mix, mods, 0, w_in0, col0=NA_WIDTH, n_cols=2 * NA_WIDTH,
                           rows_per_mod=batch * ctx_len, mod_row0=batch, tm=1024, tn=1024, name="ctx_kv")
    bias, cap = _bias_tables(rpb[0])
    attn, w_in1, w_out1, wg1, wu1, wd1 = _attention(
        proj, ctx_kv, bias, cap, batch, seq, ctx_len,
        cast_weights=[(odd_w_in, 0, None), (odd_w_out, 0, None)] + ffn_casts(1))
    x1, h = _even_out(attn, proj, conv_w[0], w_out0, x2d, mods, g_ffn, 0, seq, tm=512)
    x2, h = _ffn(h, wg0, wu0, wd0, x1, mods, 0, g_mix, 1, seq, tm=1024)

    u, v = _odd_in(h, w_in1, sg_ln_g[0][None], sg_ln_b[0][None], tm=512)
    b_s = jnp.broadcast_to(sg_b[0][:, :, None], (SG_GROUPS, CHUNK, CHUNK))
    x1, h = _odd_out(u, v, bf(sg_w[0]), b_s, w_out1, x2, mods, g_ffn, 1, seq, tm=512)
    out = _ffn(h, wg1, wu1, wd1, x1, mods, 1, final_norm_g[None], None, seq, tm=1024)
    return out.reshape(batch, seq, d)
```

```python
import functools

import jax
import jax.numpy as jnp
from jax import lax
from jax.experimental import pallas as pl
from jax.experimental.pallas import tpu as pltpu

D_MODEL = 2048
GRID_W = 64
NA_HEADS = 8
HEAD_DIM = 128
NA_WIDTH = NA_HEADS * HEAD_DIM
CONV_CH = D_MODEL - NA_WIDTH
WIN_ROWS = 8
WIN_COLS = 16
CHUNK = 128
SG_GROUPS = 16
EPS = 1e-6
NEG_INF = -1e30
UNMASKED_CAP = 3e38
MOD_ROWS = 16
NORM_SLICES = 4
FFN_TILE = 512
FFN_ROW_CHUNKS = 4
FFN_EPILOGUE_SLICES = 8
ROW_CHUNK = 256

F32 = jnp.float32
BF16 = jnp.bfloat16
BF16_SUBLANES = 16

MIB = 1024 * 1024
V7X_VMEM_BYTES = 64 * MIB
VMEM_CAP_BYTES = V7X_VMEM_BYTES - 5 * MIB
VMEM_SLACK_BYTES = 2 * MIB


def _nbytes(shape, dtype):
    n = jnp.dtype(dtype).itemsize
    for d in shape:
        n *= 1 if d is None else d
    return n


def _pallas(kernel_fn, *, name, grid, in_specs, args, out_specs, out_shape, scratch_shapes=(), temp_bytes):
    outs = list(zip(out_specs, out_shape)) if isinstance(out_specs, (list, tuple)) else [(out_specs, out_shape)]
    total = temp_bytes + VMEM_SLACK_BYTES
    for spec, arg in zip(in_specs, args):
        if spec.block_shape is not None:
            n_buffers = 2 if spec.pipeline_mode is None else spec.pipeline_mode.buffer_count
            total += n_buffers * _nbytes(spec.block_shape, arg.dtype)
    for spec, shape in outs:
        total += 2 * _nbytes(spec.block_shape, shape.dtype)
    for s in scratch_shapes:
        total += _nbytes(s.shape, s.dtype)
    assert total <= VMEM_CAP_BYTES, (name, total)
    return pl.pallas_call(
        kernel_fn, grid=grid, in_specs=in_specs, out_specs=out_specs, out_shape=out_shape,
        scratch_shapes=list(scratch_shapes),
        compiler_params=pltpu.CompilerParams(dimension_semantics=("arbitrary",) * len(grid),
                                             vmem_limit_bytes=total),
        name=name,
    )(*args)


def _resident_spec(shape):
    zeros = (0,) * len(shape)
    return pl.BlockSpec(shape, lambda *_: zeros, pipeline_mode=pl.Buffered(1))


def _dot(a, b):
    return jnp.dot(a, b, preferred_element_type=F32)


def _norm_mod(x, g, shift, scale):
    y = x * lax.rsqrt(jnp.mean(x * x, axis=-1, keepdims=True) + EPS)
    return (y * g) * (1.0 + scale) + shift


def _linear_step(ids, grid):
    step = ids[0]
    for axis in range(1, len(grid)):
        step = step * grid[axis] + ids[axis]
    return step


class _Cast:
    def __init__(self, stacked_w, layer, grid, col_tile=None):
        _, rows, cols = stacked_w.shape
        n_steps = 1
        for g in grid:
            n_steps *= g
        self.w = stacked_w
        self.grid = grid
        self.col_tile = col_tile
        self.n_blocks = nb = max(n for n in range(1, n_steps + 1)
                                 if rows % n == 0 and (rows // n) % BF16_SUBLANES == 0)
        rb = rows // nb
        block = lambda *ids: jnp.minimum(_linear_step(ids, grid), nb - 1)
        self.in_spec = pl.BlockSpec((None, rb, cols), lambda *ids: (layer, block(*ids), 0))
        if col_tile is None:
            self.out_shape = jax.ShapeDtypeStruct((rows, cols), BF16)
            self.out_spec = pl.BlockSpec((rb, cols), lambda *ids: (block(*ids), 0))
        else:
            nt = cols // col_tile
            self.out_shape = jax.ShapeDtypeStruct((nt, rows, col_tile), BF16)
            self.out_spec = pl.BlockSpec((nt, rb, col_tile), lambda *ids: (0, block(*ids), 0))

    def run(self, w_ref, o_ref):
        step = _linear_step([pl.program_id(a) for a in range(len(self.grid))], self.grid)

        @pl.when(step < self.n_blocks)
        def _():
            if self.col_tile is None:
                o_ref[...] = w_ref[...].astype(BF16)
            else:
                for t in range(o_ref.shape[0]):
                    o_ref[t] = w_ref[:, t * self.col_tile:(t + 1) * self.col_tile].astype(BF16)


def _with_casts(body, n_in, n_out, casts):
    n_c = len(casts)

    def kern(*refs):
        ins, rest = refs[:n_in], refs[n_in:]
        cast_ins, rest = rest[:n_c], rest[n_c:]
        outs, rest = rest[:n_out], rest[n_out:]
        cast_outs, scratch = rest[:n_c], rest[n_c:]
        body(*ins, *outs, *scratch)
        for cast, w_ref, o_ref in zip(casts, cast_ins, cast_outs):
            cast.run(w_ref, o_ref)

    return kern


def _ada_kernel(cond_ref, w_ref, b_ref, o_ref):
    a = jax.nn.silu(cond_ref[...]).astype(BF16)
    o_ref[...] = _dot(a, w_ref[...].astype(BF16)) + b_ref[...]


def _ada_mods(cond, ada_w, ada_b):
    depth, _, n = ada_w.shape
    tn = 1024
    return _pallas(
        _ada_kernel,
        name="ada_mods",
        grid=(depth, n // tn),
        in_specs=[
            pl.BlockSpec((MOD_ROWS, D_MODEL), lambda l, j: (0, 0)),
            pl.BlockSpec((None, D_MODEL, tn), lambda l, j: (l, 0, j)),
            pl.BlockSpec((None, 1, tn), lambda l, j: (l, 0, j)),
        ],
        args=(cond, ada_w, ada_b.reshape(depth, 1, n)),
        out_specs=pl.BlockSpec((None, MOD_ROWS, tn), lambda l, j: (l, 0, j)),
        out_shape=jax.ShapeDtypeStruct((depth, MOD_ROWS, n), F32),
        temp_bytes=_nbytes((D_MODEL, tn), BF16),
    )


def _mod_spec(layer, chunk, n_grid_axes):
    if n_grid_axes == 1:
        return pl.BlockSpec((None, MOD_ROWS, D_MODEL), lambda i: (layer, 0, chunk))
    return pl.BlockSpec((None, MOD_ROWS, D_MODEL), lambda i, j: (layer, 0, chunk))


def _norm_matmul_kernel(x_ref, g_ref, sh_ref, sc_ref, w_ref, o_ref, h_scr, *,
                        n_row_tiles, n_slices, tiles_per_mod, mod_row0):
    i = pl.program_id(0)
    j = pl.program_id(1)
    rs = x_ref.shape[0]

    def norm_slice():
        r0 = pl.multiple_of(jnp.minimum(j, n_slices - 1) * rs, rs)
        row = mod_row0 + i // tiles_per_mod
        h = _norm_mod(x_ref[...], g_ref[...], sh_ref[pl.ds(row, 1), :], sc_ref[pl.ds(row, 1), :])
        h_scr[i % 2, pl.ds(r0, rs), :] = h.astype(BF16)

    def matmul():
        o_ref[...] = _dot(h_scr[(i - 1) % 2], w_ref[...]).astype(o_ref.dtype)

    @pl.when(i == 0)
    def _():
        norm_slice()

    @pl.when((i > 0) & (i < n_row_tiles))
    def _():
        matmul()
        norm_slice()

    @pl.when(i == n_row_tiles)
    def _():
        matmul()


def _norm_matmul(x2d, g, mods, layer, w, *, col0, n_cols, rows_per_mod, mod_row0, tm, tn, name,
                 cast_weights=()):
    m = x2d.shape[0]
    n_i, n_j = m // tm, n_cols // tn
    n_s = min(n_j, NORM_SLICES)
    grid = (n_i + 1, n_j)
    body = functools.partial(_norm_matmul_kernel, n_row_tiles=n_i, n_slices=n_s,
                             tiles_per_mod=max(rows_per_mod // tm, 1), mod_row0=mod_row0)
    casts = [_Cast(cw, layer_, grid, col_tile) for cw, layer_, col_tile in cast_weights]
    cb0 = col0 // tn
    x_block = lambda i, j: jnp.minimum(i, n_i - 1) * n_s + jnp.minimum(j, n_s - 1)
    w_block = lambda i, j: cb0 + jnp.where(i == 0, 0, j)
    return _pallas(
        _with_casts(body, 5, 1, casts),
        name=name,
        grid=grid,
        in_specs=[
            pl.BlockSpec((tm // n_s, D_MODEL), lambda i, j: (x_block(i, j), 0)),
            pl.BlockSpec((None, 1, D_MODEL), lambda i, j: (layer, 0, 0)),
            _mod_spec(layer, 0, 2),
            _mod_spec(layer, 1, 2),
            pl.BlockSpec((D_MODEL, tn), lambda i, j: (0, w_block(i, j))),
        ] + [c.in_spec for c in casts],
        args=(x2d, g, mods, mods, w, *[c.w for c in casts]),
        out_specs=[pl.BlockSpec((tm, tn), lambda i, j: (jnp.maximum(i - 1, 0), jnp.where(i == 0, 0, j)))]
        + [c.out_spec for c in casts],
        out_shape=[jax.ShapeDtypeStruct((m, n_cols), BF16)] + [c.out_shape for c in casts],
        scratch_shapes=[pltpu.VMEM((2, tm, D_MODEL), BF16)],
        temp_bytes=_nbytes((tm, tn), F32) + 3 * _nbytes((tm // n_s, D_MODEL), F32),
    )


def _bias_kernel(rpb_ref, bias_ref, cap_ref):
    h = pl.program_id(0)
    n_dc = 2 * WIN_COLS - 1
    n_dr = 2 * WIN_ROWS - 1
    q = lax.broadcasted_iota(jnp.int32, (GRID_W, 2 * GRID_W), 0)
    lane = lax.broadcasted_iota(jnp.int32, (GRID_W, 2 * GRID_W), 1)
    second = lane >= GRID_W
    k = jnp.where(second, lane - GRID_W, lane)
    dc = jnp.clip(k - q, -(WIN_COLS - 1), WIN_COLS - 1) + (WIN_COLS - 1)
    pairs = []
    for m in range(n_dr - 1):
        acc = jnp.zeros((GRID_W, 2 * GRID_W), F32)
        for d in range(n_dc):
            val = jnp.where(second, rpb_ref[h, (m + 1) * n_dc + d], rpb_ref[h, m * n_dc + d])
            acc = jnp.where(dc == d, val, acc)
        pairs.append(acc)
    for o in range(WIN_ROWS):
        for jj in range(WIN_ROWS // 2):
            bias_ref[o, :, jj * 2 * GRID_W:(jj + 1) * 2 * GRID_W] = pairs[o + 2 * jj]
    col_start = jnp.clip(q - WIN_COLS // 2, 0, GRID_W - WIN_COLS)
    cap = jnp.where((k >= col_start) & (k < col_start + WIN_COLS), UNMASKED_CAP, NEG_INF).astype(F32)
    for jj in range(WIN_ROWS // 2):
        cap_ref[:, jj * 2 * GRID_W:(jj + 1) * 2 * GRID_W] = cap


def _bias_tables(rpb):
    h = rpb.shape[0]
    kw = WIN_ROWS * GRID_W
    return _pallas(
        _bias_kernel,
        name="bias_tables",
        grid=(h,),
        in_specs=[pl.BlockSpec(memory_space=pltpu.SMEM)],
        args=(rpb.reshape(h, -1),),
        out_specs=[
            pl.BlockSpec((None, WIN_ROWS, GRID_W, kw), lambda i: (i, 0, 0, 0)),
            pl.BlockSpec((GRID_W, kw), lambda i: (0, 0)),
        ],
        out_shape=[
            jax.ShapeDtypeStruct((h, WIN_ROWS, GRID_W, kw), F32),
            jax.ShapeDtypeStruct((GRID_W, kw), F32),
        ],
        temp_bytes=2 * WIN_ROWS * _nbytes((GRID_W, 2 * GRID_W), F32),
    )


def _attn_kernel(q_ref, k_ref, v_ref, kc_ref, vc_ref, bias_ref, cap_ref, o_ref,
                 s_scr, e_scr, vx_scr, vcx_scr, acc_scr, *, rows):
    scale = HEAD_DIM ** -0.5
    kw = WIN_ROWS * GRID_W
    ctx_len = kc_ref.shape[0]
    nt = (((1,), (1,)), ((), ()))

    vx_scr[:, :HEAD_DIM] = v_ref[...]
    vx_scr[:, HEAD_DIM:] = jnp.ones((v_ref.shape[0], HEAD_DIM), BF16)
    vcx_scr[:, :HEAD_DIM] = vc_ref[...]
    vcx_scr[:, HEAD_DIM:] = jnp.ones((ctx_len, HEAD_DIM), BF16)

    s_scr[:, kw:] = lax.dot_general(q_ref[...], kc_ref[...], nt, preferred_element_type=F32) * scale

    def scores(r, carry):
        row_start = jnp.clip(r - WIN_ROWS // 2, 0, rows - WIN_ROWS)
        o = row_start - r + (WIN_ROWS - 1)
        q0 = pl.multiple_of(r * GRID_W, GRID_W)
        k0 = pl.multiple_of(row_start * GRID_W, GRID_W)
        s = lax.dot_general(q_ref[pl.ds(q0, GRID_W), :], k_ref[pl.ds(k0, kw), :], nt,
                            preferred_element_type=F32)
        s_scr[pl.ds(q0, GRID_W), :kw] = jnp.minimum(s * scale + bias_ref[o], cap_ref[...])
        return carry

    lax.fori_loop(0, rows, scores, 0, unroll=16)

    def expo(r, carry):
        q0 = pl.multiple_of(r * GRID_W, GRID_W)
        s = s_scr[pl.ds(q0, GRID_W), :]
        e_scr[pl.ds(q0, GRID_W), :] = jnp.exp(s - jnp.max(s, axis=-1, keepdims=True)).astype(BF16)
        return carry

    lax.fori_loop(0, rows, expo, 0, unroll=16)

    acc_scr[...] = _dot(e_scr[:, kw:], vcx_scr[...])

    def values(r, carry):
        row_start = jnp.clip(r - WIN_ROWS // 2, 0, rows - WIN_ROWS)
        q0 = pl.multiple_of(r * GRID_W, GRID_W)
        k0 = pl.multiple_of(row_start * GRID_W, GRID_W)
        t = acc_scr[pl.ds(q0, GRID_W), :] + _dot(e_scr[pl.ds(q0, GRID_W), :kw], vx_scr[pl.ds(k0, kw), :])
        o_ref[pl.ds(q0, GRID_W), :] = (t[:, :HEAD_DIM] / t[:, HEAD_DIM:]).astype(o_ref.dtype)
        return carry

    lax.fori_loop(0, rows, values, 0, unroll=16)


def _attention(qkv, ctx_kv, bias, cap, batch, seq, ctx_len, cast_weights=()):
    rows = seq // GRID_W
    kw = WIN_ROWS * GRID_W
    grid = (NA_HEADS, batch)
    casts = [_Cast(cw, layer_, grid, col_tile) for cw, layer_, col_tile in cast_weights]
    return _pallas(
        _with_casts(functools.partial(_attn_kernel, rows=rows), 7, 1, casts),
        name="nbr_attention",
        grid=grid,
        in_specs=[
            pl.BlockSpec((seq, HEAD_DIM), lambda h, b: (b, h)),
            pl.BlockSpec((seq, HEAD_DIM), lambda h, b: (b, NA_HEADS + h)),
            pl.BlockSpec((seq, HEAD_DIM), lambda h, b: (b, 2 * NA_HEADS + h)),
            pl.BlockSpec((ctx_len, HEAD_DIM), lambda h, b: (b, h)),
            pl.BlockSpec((ctx_len, HEAD_DIM), lambda h, b: (b, NA_HEADS + h)),
            pl.BlockSpec((None, WIN_ROWS, GRID_W, kw), lambda h, b: (h, 0, 0, 0)),
            pl.BlockSpec((GRID_W, kw), lambda h, b: (0, 0)),
        ] + [c.in_spec for c in casts],
        args=(qkv, qkv, qkv, ctx_kv, ctx_kv, bias, cap, *[c.w for c in casts]),
        out_specs=[pl.BlockSpec((seq, HEAD_DIM), lambda h, b: (b, h))] + [c.out_spec for c in casts],
        out_shape=[jax.ShapeDtypeStruct((batch * seq, NA_WIDTH), BF16)] + [c.out_shape for c in casts],
        scratch_shapes=[
            pltpu.VMEM((seq, kw + ctx_len), F32),
            pltpu.VMEM((seq, kw + ctx_len), BF16),
            pltpu.VMEM((seq, 2 * HEAD_DIM), BF16),
            pltpu.VMEM((ctx_len, 2 * HEAD_DIM), BF16),
            pltpu.VMEM((seq, 2 * HEAD_DIM), F32),
        ],
        temp_bytes=_nbytes((seq, ctx_len), F32),
    )


def _residual_and_ffn_norm(rows, mod_row, x_ref, mix, gate_ref, gn_ref, shf_ref, scf_ref, x1_ref, h_ref):
    x1 = x_ref[rows, :] + gate_ref[pl.ds(mod_row, 1), :] * mix
    x1_ref[rows, :] = x1
    h = _norm_mod(x1, gn_ref[...], shf_ref[pl.ds(mod_row, 1), :], scf_ref[pl.ds(mod_row, 1), :])
    h_ref[rows, :] = h.astype(BF16)


def _row_chunks(tm):
    return [slice(r, r + ROW_CHUNK) for r in range(0, tm, ROW_CHUNK)]


def _skewed(chunks, matmul_fn, epilogue_fn):
    pending = None
    for rows in chunks:
        acc = matmul_fn(rows)
        if pending is not None:
            epilogue_fn(*pending)
        pending = (rows, acc)
    epilogue_fn(*pending)


def _even_out_kernel(attn_ref, b_ref, c_ref, u_ref, cp_ref, up_ref, cn_ref, un_ref, cw_ref, wo_ref,
                     x_ref, gate_ref, gn_ref, shf_ref, scf_ref, x1_ref, h_ref, conv_scr, *, tiles_per_mod, halo):
    i = pl.program_id(0)
    tm = c_ref.shape[0]
    t = i % tiles_per_mod
    conv_scr[:, :NA_WIDTH] = attn_ref[...]
    z = c_ref[...].astype(F32) * u_ref[...].astype(F32)
    z_before = cp_ref[halo - 1:halo, :].astype(F32) * up_ref[halo - 1:halo, :].astype(F32)
    z_after = cn_ref[0:1, :].astype(F32) * un_ref[0:1, :].astype(F32)
    z_before = jnp.where(t == 0, 0.0, z_before)
    z_after = jnp.where(t == tiles_per_mod - 1, 0.0, z_after)
    ridx = lax.broadcasted_iota(jnp.int32, (tm, 1), 0)
    z_prev = jnp.where(ridx == 0, z_before, pltpu.roll(z, 1, 0))
    z_next = jnp.where(ridx == tm - 1, z_after, pltpu.roll(z, tm - 1, 0))
    y = z_prev * cw_ref[0:1, :] + z * cw_ref[1:2, :] + z_next * cw_ref[2:3, :]
    conv_scr[:, NA_WIDTH:] = (b_ref[...].astype(F32) * y).astype(BF16)
    _skewed(
        _row_chunks(tm),
        lambda rows: _dot(conv_scr[rows, :], wo_ref[...]),
        lambda rows, mix: _residual_and_ffn_norm(rows, i // tiles_per_mod, x_ref, mix, gate_ref, gn_ref,
                                                 shf_ref, scf_ref, x1_ref, h_ref))


def _even_out(attn, proj, conv_w, w_out, x2d, mods, norm_ffn_g, layer, seq, tm):
    m = x2d.shape[0]
    halo = 16
    hb = tm // halo
    n_halo_blocks = m // halo
    cb = NA_WIDTH * 3 // CONV_CH
    kern = functools.partial(_even_out_kernel, tiles_per_mod=seq // tm, halo=halo)
    main = lambda c: pl.BlockSpec((tm, CONV_CH), lambda i: (i, c))
    prev = lambda c: pl.BlockSpec((halo, CONV_CH), lambda i: (jnp.maximum(i * hb - 1, 0), c))
    nxt = lambda c: pl.BlockSpec((halo, CONV_CH), lambda i: (jnp.minimum((i + 1) * hb, n_halo_blocks - 1), c))
    return _pallas(
        kern,
        name="even_out",
        grid=(m // tm,),
        in_specs=[
            pl.BlockSpec((tm, NA_WIDTH), lambda i: (i, 0)),
            main(cb), main(cb + 1), main(cb + 2),
            prev(cb + 1), prev(cb + 2), nxt(cb + 1), nxt(cb + 2),
            _resident_spec(conv_w.shape),
            _resident_spec(w_out.shape),
            pl.BlockSpec((tm, D_MODEL), lambda i: (i, 0)),
            _mod_spec(layer, 2, 1),
            pl.BlockSpec((None, 1, D_MODEL), lambda i: (layer, 0, 0)),
            _mod_spec(layer, 3, 1),
            _mod_spec(layer, 4, 1),
        ],
        args=(attn, proj, proj, proj, proj, proj, proj, proj, conv_w, w_out, x2d, mods, norm_ffn_g, mods, mods),
        out_specs=[
            pl.BlockSpec((tm, D_MODEL), lambda i: (i, 0)),
            pl.BlockSpec((tm, D_MODEL), lambda i: (i, 0)),
        ],
        out_shape=[
            jax.ShapeDtypeStruct((m, D_MODEL), F32),
            jax.ShapeDtypeStruct((m, D_MODEL), BF16),
        ],
        scratch_shapes=[pltpu.VMEM((tm, NA_WIDTH + CONV_CH), BF16)],
        temp_bytes=3 * _nbytes((tm, CONV_CH), F32) + 2 * _nbytes((tm, D_MODEL), F32),
    )


def _ffn_kernel(*refs, n_row_tiles, tiles_per_mod, modulated):
    if modulated:
        (h_ref, wg_ref, wu_ref, wd_ref, x_ref, gate_ref, gn_ref, sh_ref, sc_ref,
         x2_ref, y_ref, acc_ref) = refs
    else:
        h_ref, wg_ref, wu_ref, wd_ref, x_ref, gate_ref, gn_ref, y_ref, acc_ref = refs
    i = pl.program_id(0)
    f = pl.program_id(1)
    slot = i % 2
    tm = h_ref.shape[0]
    rs = tm // FFN_EPILOGUE_SLICES

    @pl.when((f == 0) & (i < n_row_tiles))
    def _():
        acc_ref[slot] = jnp.zeros((tm, D_MODEL), F32)

    def matmuls():
        rc = tm // FFN_ROW_CHUNKS
        ts = []
        for c in range(FFN_ROW_CHUNKS):
            h = h_ref[c * rc:(c + 1) * rc, :]
            ts.append((jax.nn.silu(_dot(h, wg_ref[...])) * _dot(h, wu_ref[...])).astype(BF16))
        for c in range(FFN_ROW_CHUNKS):
            acc_ref[slot, c * rc:(c + 1) * rc, :] += _dot(ts[c], wd_ref[...])

    def epilogue_slice():
        r0 = pl.multiple_of(jnp.minimum(f, FFN_EPILOGUE_SLICES - 1) * rs, rs)
        row = (i - 1) // tiles_per_mod
        x2 = x_ref[...] + gate_ref[pl.ds(row, 1), :] * acc_ref[1 - slot, pl.ds(r0, rs), :]
        if modulated:
            x2_ref[...] = x2
            y = _norm_mod(x2, gn_ref[...], sh_ref[pl.ds(row, 1), :], sc_ref[pl.ds(row, 1), :])
        else:
            y = x2 * lax.rsqrt(jnp.mean(x2 * x2, axis=-1, keepdims=True) + EPS) * gn_ref[...]
        y_ref[...] = y.astype(y_ref.dtype)

    @pl.when(i == 0)
    def _():
        matmuls()

    @pl.when((i > 0) & (i < n_row_tiles))
    def _():
        epilogue_slice()
        matmuls()

    @pl.when(i == n_row_tiles)
    def _():
        epilogue_slice()


def _ffn(h, w_gate, w_up, w_down, x2d, mods, layer, next_g, next_layer, seq, tm):
    m = x2d.shape[0]
    n_f, _, tf = w_gate.shape
    n_i = m // tm
    n_s = FFN_EPILOGUE_SLICES
    assert n_f >= n_s and (tm // n_s) % BF16_SUBLANES == 0
    modulated = next_layer is not None
    kern = functools.partial(_ffn_kernel, n_row_tiles=n_i, tiles_per_mod=seq // tm, modulated=modulated)
    w_tile = lambda i, f: jnp.where(i == n_i, n_f - 1, f)
    e_block = lambda i, f: jnp.where(i == 0, 0, (i - 1) * n_s + jnp.minimum(f, n_s - 1))
    slice_spec = pl.BlockSpec((tm // n_s, D_MODEL), lambda i, f: (e_block(i, f), 0))
    in_specs = [
        pl.BlockSpec((tm, D_MODEL), lambda i, f: (jnp.minimum(i, n_i - 1), 0)),
        pl.BlockSpec((None, D_MODEL, tf), lambda i, f: (w_tile(i, f), 0, 0)),
        pl.BlockSpec((None, D_MODEL, tf), lambda i, f: (w_tile(i, f), 0, 0)),
        pl.BlockSpec((tf, D_MODEL), lambda i, f: (w_tile(i, f), 0)),
        slice_spec,
        _mod_spec(layer, 5, 2),
    ]
    args = [h, w_gate, w_up, w_down, x2d, mods]
    if modulated:
        in_specs += [pl.BlockSpec((None, 1, D_MODEL), lambda i, f: (next_layer, 0, 0)),
                     _mod_spec(next_layer, 0, 2), _mod_spec(next_layer, 1, 2)]
        args += [next_g, mods, mods]
        out_specs = [slice_spec, slice_spec]
        out_shape = [jax.ShapeDtypeStruct((m, D_MODEL), F32), jax.ShapeDtypeStruct((m, D_MODEL), BF16)]
    else:
        in_specs += [pl.BlockSpec((1, D_MODEL), lambda i, f: (0, 0))]
        args += [next_g]
        out_specs = slice_spec
        out_shape = jax.ShapeDtypeStruct((m, D_MODEL), F32)
    rc = tm // FFN_ROW_CHUNKS
    return _pallas(
        kern,
        name="ffn_final" if not modulated else "ffn",
        grid=(n_i + 1, n_f),
        in_specs=in_specs,
        args=args,
        out_specs=out_specs,
        out_shape=out_shape,
        scratch_shapes=[pltpu.VMEM((2, tm, D_MODEL), F32)],
        temp_bytes=FFN_ROW_CHUNKS * (2 * _nbytes((rc, tf), F32) + _nbytes((rc, tf), BF16))
        + _nbytes((rc, D_MODEL), F32),
    )


def _odd_in_kernel(h_ref, w_ref, lg_ref, lb_ref, u_ref, v_ref):
    def matmuls(rows):
        h = h_ref[rows, :]
        return _dot(h, w_ref[:, :D_MODEL]), _dot(h, w_ref[:, D_MODEL:])

    def epilogue(rows, acc):
        u_ref[rows, :] = jax.nn.gelu(acc[0], approximate=True).astype(BF16)
        z = jax.nn.gelu(acc[1], approximate=True)
        zc = z - jnp.mean(z, axis=-1, keepdims=True)
        var = jnp.mean(zc * zc, axis=-1, keepdims=True)
        v_ref[rows, :] = (zc * lax.rsqrt(var + EPS) * lg_ref[...] + lb_ref[...]).astype(BF16)

    _skewed(_row_chunks(h_ref.shape[0]), matmuls, epilogue)


def _odd_in(h, w_in, ln_g, ln_b, tm):
    m = h.shape[0]
    row_spec = pl.BlockSpec((tm, D_MODEL), lambda i: (i, 0))
    return _pallas(
        _odd_in_kernel,
        name="odd_in",
        grid=(m // tm,),
        in_specs=[
            row_spec,
            _resident_spec(w_in.shape),
            pl.BlockSpec((1, D_MODEL), lambda i: (0, 0)),
            pl.BlockSpec((1, D_MODEL), lambda i: (0, 0)),
        ],
        args=(h, w_in, ln_g, ln_b),
        out_specs=[row_spec, row_spec],
        out_shape=[jax.ShapeDtypeStruct((m, D_MODEL), BF16)] * 2,
        temp_bytes=2 * 4 * _nbytes((ROW_CHUNK, D_MODEL), F32),
    )


def _odd_out_kernel(u_ref, v_ref, ws_ref, bs_ref, wo_ref, x_ref, gate_ref, gn_ref, shf_ref, scf_ref,
                    x1_ref, h_ref, t_scr, *, tiles_per_mod):
    i = pl.program_id(0)
    n_chunks = u_ref.shape[0] // CHUNK
    for g in range(SG_GROUPS):
        cols = slice(g * CHUNK, (g + 1) * CHUNK)
        vcat = jnp.concatenate([v_ref[n * CHUNK:(n + 1) * CHUNK, cols] for n in range(n_chunks)], axis=1)
        mixed = _dot(ws_ref[g], vcat)
        for n in range(n_chunks):
            rows = slice(n * CHUNK, (n + 1) * CHUNK)
            gated = u_ref[rows, cols].astype(F32) * (mixed[:, n * CHUNK:(n + 1) * CHUNK] + bs_ref[g])
            t_scr[rows, cols] = gated.astype(BF16)
    _skewed(
        _row_chunks(u_ref.shape[0]),
        lambda rows: _dot(t_scr[rows, :], wo_ref[...]),
        lambda rows, mix: _residual_and_ffn_norm(rows, i // tiles_per_mod, x_ref, mix, gate_ref, gn_ref,
                                                 shf_ref, scf_ref, x1_ref, h_ref))


def _odd_out(u, v, w_s, b_s, w_out, x2d, mods, norm_ffn_g, layer, seq, tm):
    m = x2d.shape[0]
    kern = functools.partial(_odd_out_kernel, tiles_per_mod=seq // tm)
    return _pallas(
        kern,
        name="odd_out",
        grid=(m // tm,),
        in_specs=[
            pl.BlockSpec((tm, D_MODEL), lambda i: (i, 0)),
            pl.BlockSpec((tm, D_MODEL), lambda i: (i, 0)),
            _resident_spec(w_s.shape),
            _resident_spec(b_s.shape),
            _resident_spec(w_out.shape),
            pl.BlockSpec((tm, D_MODEL), lambda i: (i, 0)),
            _mod_spec(layer, 2, 1),
            pl.BlockSpec((None, 1, D_MODEL), lambda i: (layer, 0, 0)),
            _mod_spec(layer, 3, 1),
            _mod_spec(layer, 4, 1),
        ],
        args=(u, v, w_s, b_s, w_out, x2d, mods, norm_ffn_g, mods, mods),
        out_specs=[
            pl.BlockSpec((tm, D_MODEL), lambda i: (i, 0)),
            pl.BlockSpec((tm, D_MODEL), lambda i: (i, 0)),
        ],
        out_shape=[
            jax.ShapeDtypeStruct((m, D_MODEL), F32),
            jax.ShapeDtypeStruct((m, D_MODEL), BF16),
        ],
        scratch_shapes=[pltpu.VMEM((tm, D_MODEL), BF16)],
        temp_bytes=2 * 2 * _nbytes((ROW_CHUNK, D_MODEL), F32),
    )


def kernel(x, c, ctx, c_ctx, ada_w, ada_b, norm_mix_g, norm_ffn_g, ffn_w_gate, ffn_w_up, ffn_w_down,
           even_w_in, even_w_out, rpb, conv_w, odd_w_in, odd_w_out, sg_ln_g, sg_ln_b, sg_w, sg_b,
           final_norm_g):
    batch, seq, d = x.shape
    ctx_len = ctx.shape[1]
    depth = ada_w.shape[0]
    assert d == D_MODEL and depth == 2 and batch < MOD_ROWS and seq % (GRID_W * WIN_ROWS) == 0

    bf = lambda w: w.astype(BF16)
    x2d = x.reshape(batch * seq, d)
    ctx2d = ctx.reshape(batch * ctx_len, d)
    g_mix = norm_mix_g.reshape(depth, 1, d)
    g_ffn = norm_ffn_g.reshape(depth, 1, d)

    cond = jnp.concatenate([c, c_ctx[None], jnp.zeros((MOD_ROWS - batch - 1, d), F32)], axis=0)
    mods = _ada_mods(cond, ada_w, ada_b)

    ffn_casts = lambda layer: [(ffn_w_gate, layer, FFN_TILE), (ffn_w_up, layer, FFN_TILE),
                               (ffn_w_down, layer, None)]
    w_in0 = bf(even_w_in[0])
    proj, w_out0, wg0, wu0, wd0 = _norm_matmul(
        x2d, g_mix, mods, 0, w_in0, col0=0, n_cols=w_in0.shape[1], rows_per_mod=seq, mod_row0=0,
        tm=1024, tn=1536, name="even_in", cast_weights=[(even_w_out, 0, None)] + ffn_casts(0))
    ctx_kv, = _norm_matmul(ctx2d, g_mix, mods, 0, w_in0, col0=NA_WIDTH, n_cols=2 * NA_WIDTH,
                           rows_per_mod=batch * ctx_len, mod_row0=batch, tm=1024, tn=1024, name="ctx_kv")
    bias, cap = _bias_tables(rpb[0])
    attn, w_in1, w_out1, wg1, wu1, wd1 = _attention(
        proj, ctx_kv, bias, cap, batch, seq, ctx_len,
        cast_weights=[(odd_w_in, 0, None), (odd_w_out, 0, None)] + ffn_casts(1))
    x1, h = _even_out(attn, proj, conv_w[0], w_out0, x2d, mods, g_ffn, 0, seq, tm=512)
    x2, h = _ffn(h, wg0, wu0, wd0, x1, mods, 0, g_mix, 1, seq, tm=1024)

    u, v = _odd_in(h, w_in1, sg_ln_g[0][None], sg_ln_b[0][None], tm=512)
    b_s = jnp.broadcast_to(sg_b[0][:, :, None], (SG_GROUPS, CHUNK, CHUNK))
    x1, h = _odd_out(u, v, bf(sg_w[0]), b_s, w_out1, x2, mods, g_ffn, 1, seq, tm=512)
    out = _ffn(h, wg1, wu1, wd1, x1, mods, 1, final_norm_g[None], None, seq, tm=1024)
    return out.reshape(batch, seq, d)
```

```python
import functools

import jax
import jax.numpy as jnp
from jax import lax
from jax.experimental import pallas as pl
from jax.experimental.pallas import tpu as pltpu

D_MODEL = 2048
GRID_W = 64
NA_HEADS = 8
HEAD_DIM = 128
NA_WIDTH = NA_HEADS * HEAD_DIM
CONV_CH = D_MODEL - NA_WIDTH
WIN_ROWS = 8
WIN_COLS = 16
CHUNK = 128
SG_GROUPS = 16
EPS = 1e-6
NEG_INF = -1e30
UNMASKED_CAP = 3e38
MOD_ROWS = 16
ATTN_HEADS_PER_STEP = 2
NORM_SLICES = 4
FFN_TILE = 512
FFN_ROW_CHUNKS = 4
FFN_EPILOGUE_SLICES = 8
ROW_CHUNK = 256

F32 = jnp.float32
BF16 = jnp.bfloat16
BF16_SUBLANES = 16

MIB = 1024 * 1024
V7X_VMEM_BYTES = 64 * MIB
VMEM_CAP_BYTES = V7X_VMEM_BYTES - 5 * MIB
VMEM_SLACK_BYTES = 2 * MIB


def _nbytes(shape, dtype):
    n = jnp.dtype(dtype).itemsize
    for d in shape:
        n *= 1 if d is None else d
    return n


def _pallas(kernel_fn, *, name, grid, in_specs, args, out_specs, out_shape, scratch_shapes=(), temp_bytes):
    outs = list(zip(out_specs, out_shape)) if isinstance(out_specs, (list, tuple)) else [(out_specs, out_shape)]
    total = temp_bytes + VMEM_SLACK_BYTES
    for spec, arg in zip(in_specs, args):
        if spec.block_shape is not None:
            n_buffers = 2 if spec.pipeline_mode is None else spec.pipeline_mode.buffer_count
            total += n_buffers * _nbytes(spec.block_shape, arg.dtype)
    for spec, shape in outs:
        total += 2 * _nbytes(spec.block_shape, shape.dtype)
    for s in scratch_shapes:
        total += _nbytes(s.shape, s.dtype)
    assert total <= VMEM_CAP_BYTES, (name, total)
    return pl.pallas_call(
        kernel_fn, grid=grid, in_specs=in_specs, out_specs=out_specs, out_shape=out_shape,
        scratch_shapes=list(scratch_shapes),
        compiler_params=pltpu.CompilerParams(dimension_semantics=("arbitrary",) * len(grid),
                                             vmem_limit_bytes=total),
        name=name,
    )(*args)


def _resident_spec(shape):
    zeros = (0,) * len(shape)
    return pl.BlockSpec(shape, lambda *_: zeros, pipeline_mode=pl.Buffered(1))


def _dot(a, b):
    return jnp.dot(a, b, preferred_element_type=F32)


def _norm_mod(x, g, shift, scale):
    y = x * lax.rsqrt(jnp.mean(x * x, axis=-1, keepdims=True) + EPS)
    return (y * g) * (1.0 + scale) + shift


def _linear_step(ids, grid):
    step = ids[0]
    for axis in range(1, len(grid)):
        step = step * grid[axis] + ids[axis]
    return step


class _Cast:
    def __init__(self, stacked_w, layer, grid, col_tile=None):
        _, rows, cols = stacked_w.shape
        n_steps = 1
        for g in grid:
            n_steps *= g
        self.w = stacked_w
        self.grid = grid
        self.col_tile = col_tile
        self.n_blocks = nb = max(n for n in range(1, n_steps + 1)
                                 if rows % n == 0 and (rows // n) % BF16_SUBLANES == 0)
        rb = rows // nb
        block = lambda *ids: jnp.minimum(_linear_step(ids, grid), nb - 1)
        self.in_spec = pl.BlockSpec((None, rb, cols), lambda *ids: (layer, block(*ids), 0))
        if col_tile is None:
            self.out_shape = jax.ShapeDtypeStruct((rows, cols), BF16)
            self.out_spec = pl.BlockSpec((rb, cols), lambda *ids: (block(*ids), 0))
        else:
            nt = cols // col_tile
            self.out_shape = jax.ShapeDtypeStruct((nt, rows, col_tile), BF16)
            self.out_spec = pl.BlockSpec((nt, rb, col_tile), lambda *ids: (0, block(*ids), 0))

    def run(self, w_ref, o_ref):
        step = _linear_step([pl.program_id(a) for a in range(len(self.grid))], self.grid)

        @pl.when(step < self.n_blocks)
        def _():
            if self.col_tile is None:
                o_ref[...] = w_ref[...].astype(BF16)
            else:
                for t in range(o_ref.shape[0]):
                    o_ref[t] = w_ref[:, t * self.col_tile:(t + 1) * self.col_tile].astype(BF16)


def _with_casts(body, n_in, n_out, casts):
    n_c = len(casts)

    def kern(*refs):
        ins, rest = refs[:n_in], refs[n_in:]
        cast_ins, rest = rest[:n_c], rest[n_c:]
        outs, rest = rest[:n_out], rest[n_out:]
        cast_outs, scratch = rest[:n_c], rest[n_c:]
        body(*ins, *outs, *scratch)
        for cast, w_ref, o_ref in zip(casts, cast_ins, cast_outs):
            cast.run(w_ref, o_ref)

    return kern


def _ada_kernel(cond_ref, w_ref, b_ref, o_ref):
    a = jax.nn.silu(cond_ref[...]).astype(BF16)
    o_ref[...] = _dot(a, w_ref[...].astype(BF16)) + b_ref[...]


def _ada_mods(cond, ada_w, ada_b):
    depth, _, n = ada_w.shape
    tn = 1024
    return _pallas(
        _ada_kernel,
        name="ada_mods",
        grid=(depth, n // tn),
        in_specs=[
            pl.BlockSpec((MOD_ROWS, D_MODEL), lambda l, j: (0, 0)),
            pl.BlockSpec((None, D_MODEL, tn), lambda l, j: (l, 0, j)),
            pl.BlockSpec((None, 1, tn), lambda l, j: (l, 0, j)),
        ],
        args=(cond, ada_w, ada_b.reshape(depth, 1, n)),
        out_specs=pl.BlockSpec((None, MOD_ROWS, tn), lambda l, j: (l, 0, j)),
        out_shape=jax.ShapeDtypeStruct((depth, MOD_ROWS, n), F32),
        temp_bytes=_nbytes((D_MODEL, tn), BF16),
    )


def _mod_spec(layer, chunk, n_grid_axes):
    if n_grid_axes == 1:
        return pl.BlockSpec((None, MOD_ROWS, D_MODEL), lambda i: (layer, 0, chunk))
    return pl.BlockSpec((None, MOD_ROWS, D_MODEL), lambda i, j: (layer, 0, chunk))


def _norm_matmul_kernel(x_ref, g_ref, sh_ref, sc_ref, w_ref, o_ref, h_scr, *,
                        n_row_tiles, n_slices, tiles_per_mod, mod_row0):
    i = pl.program_id(0)
    j = pl.program_id(1)
    rs = x_ref.shape[0]

    def norm_slice():
        r0 = pl.multiple_of(jnp.minimum(j, n_slices - 1) * rs, rs)
        row = mod_row0 + i // tiles_per_mod
        h = _norm_mod(x_ref[...], g_ref[...], sh_ref[pl.ds(row, 1), :], sc_ref[pl.ds(row, 1), :])
        h_scr[i % 2, pl.ds(r0, rs), :] = h.astype(BF16)

    def matmul():
        o_ref[...] = _dot(h_scr[(i - 1) % 2], w_ref[...]).astype(o_ref.dtype)

    @pl.when(i == 0)
    def _():
        norm_slice()

    @pl.when((i > 0) & (i < n_row_tiles))
    def _():
        matmul()
        norm_slice()

    @pl.when(i == n_row_tiles)
    def _():
        matmul()


def _norm_matmul(x2d, g, mods, layer, w, *, col0, n_cols, rows_per_mod, mod_row0, tm, tn, name,
                 cast_weights=()):
    m = x2d.shape[0]
    n_i, n_j = m // tm, n_cols // tn
    n_s = min(n_j, NORM_SLICES)
    grid = (n_i + 1, n_j)
    body = functools.partial(_norm_matmul_kernel, n_row_tiles=n_i, n_slices=n_s,
                             tiles_per_mod=max(rows_per_mod // tm, 1), mod_row0=mod_row0)
    casts = [_Cast(cw, layer_, grid, col_tile) for cw, layer_, col_tile in cast_weights]
    cb0 = col0 // tn
    x_block = lambda i, j: jnp.minimum(i, n_i - 1) * n_s + jnp.minimum(j, n_s - 1)
    w_block = lambda i, j: cb0 + jnp.where(i == 0, 0, j)
    return _pallas(
        _with_casts(body, 5, 1, casts),
        name=name,
        grid=grid,
        in_specs=[
            pl.BlockSpec((tm // n_s, D_MODEL), lambda i, j: (x_block(i, j), 0)),
            pl.BlockSpec((None, 1, D_MODEL), lambda i, j: (layer, 0, 0)),
            _mod_spec(layer, 0, 2),
            _mod_spec(layer, 1, 2),
            pl.BlockSpec((D_MODEL, tn), lambda i, j: (0, w_block(i, j))),
        ] + [c.in_spec for c in casts],
        args=(x2d, g, mods, mods, w, *[c.w for c in casts]),
        out_specs=[pl.BlockSpec((tm, tn), lambda i, j: (jnp.maximum(i - 1, 0), jnp.where(i == 0, 0, j)))]
        + [c.out_spec for c in casts],
        out_shape=[jax.ShapeDtypeStruct((m, n_cols), BF16)] + [c.out_shape for c in casts],
        scratch_shapes=[pltpu.VMEM((2, tm, D_MODEL), BF16)],
        temp_bytes=_nbytes((tm, tn), F32) + 3 * _nbytes((tm // n_s, D_MODEL), F32),
    )


def _bias_kernel(rpb_ref, bias_ref, cap_ref):
    h = pl.program_id(0)
    n_dc = 2 * WIN_COLS - 1
    n_dr = 2 * WIN_ROWS - 1
    q = lax.broadcasted_iota(jnp.int32, (GRID_W, 2 * GRID_W), 0)
    lane = lax.broadcasted_iota(jnp.int32, (GRID_W, 2 * GRID_W), 1)
    second = lane >= GRID_W
    k = jnp.where(second, lane - GRID_W, lane)
    dc = jnp.clip(k - q, -(WIN_COLS - 1), WIN_COLS - 1) + (WIN_COLS - 1)
    pairs = []
    for m in range(n_dr - 1):
        acc = jnp.zeros((GRID_W, 2 * GRID_W), F32)
        for d in range(n_dc):
            val = jnp.where(second, rpb_ref[h, (m + 1) * n_dc + d], rpb_ref[h, m * n_dc + d])
            acc = jnp.where(dc == d, val, acc)
        pairs.append(acc)
    for o in range(WIN_ROWS):
        for jj in range(WIN_ROWS // 2):
            bias_ref[o, :, jj * 2 * GRID_W:(jj + 1) * 2 * GRID_W] = pairs[o + 2 * jj]
    col_start = jnp.clip(q - WIN_COLS // 2, 0, GRID_W - WIN_COLS)
    cap = jnp.where((k >= col_start) & (k < col_start + WIN_COLS), UNMASKED_CAP, NEG_INF).astype(F32)
    for jj in range(WIN_ROWS // 2):
        cap_ref[:, jj * 2 * GRID_W:(jj + 1) * 2 * GRID_W] = cap


def _bias_tables(rpb):
    h = rpb.shape[0]
    kw = WIN_ROWS * GRID_W
    return _pallas(
        _bias_kernel,
        name="bias_tables",
        grid=(h,),
        in_specs=[pl.BlockSpec(memory_space=pltpu.SMEM)],
        args=(rpb.reshape(h, -1),),
        out_specs=[
            pl.BlockSpec((None, WIN_ROWS, GRID_W, kw), lambda i: (i, 0, 0, 0)),
            pl.BlockSpec((GRID_W, kw), lambda i: (0, 0)),
        ],
        out_shape=[
            jax.ShapeDtypeStruct((h, WIN_ROWS, GRID_W, kw), F32),
            jax.ShapeDtypeStruct((GRID_W, kw), F32),
        ],
        temp_bytes=2 * WIN_ROWS * _nbytes((GRID_W, 2 * GRID_W), F32),
    )


def _attn_kernel(q_ref, k_ref, v_ref, kc_ref, vc_ref, bias_ref, cap_ref, o_ref,
                 s_scr, e_scr, vx_scr, vcx_scr, acc_scr, *, rows):
    scale = HEAD_DIM ** -0.5
    kw = WIN_ROWS * GRID_W
    ctx_len = kc_ref.shape[0]
    nt = (((1,), (1,)), ((), ()))
    vx_scr[:, HEAD_DIM:] = jnp.ones((v_ref.shape[0], HEAD_DIM), BF16)
    vcx_scr[:, HEAD_DIM:] = jnp.ones((ctx_len, HEAD_DIM), BF16)

    for head in range(ATTN_HEADS_PER_STEP):
        lanes = slice(head * HEAD_DIM, (head + 1) * HEAD_DIM)
        vx_scr[:, :HEAD_DIM] = v_ref[:, lanes]
        vcx_scr[:, :HEAD_DIM] = vc_ref[:, lanes]

        s_scr[:, kw:] = lax.dot_general(q_ref[:, lanes], kc_ref[:, lanes], nt,
                                        preferred_element_type=F32) * scale

        def scores(r, carry):
            row_start = jnp.clip(r - WIN_ROWS // 2, 0, rows - WIN_ROWS)
            o = row_start - r + (WIN_ROWS - 1)
            q0 = pl.multiple_of(r * GRID_W, GRID_W)
            k0 = pl.multiple_of(row_start * GRID_W, GRID_W)
            s = lax.dot_general(q_ref[pl.ds(q0, GRID_W), lanes], k_ref[pl.ds(k0, kw), lanes], nt,
                                preferred_element_type=F32)
            s_scr[pl.ds(q0, GRID_W), :kw] = jnp.minimum(s * scale + bias_ref[head, o], cap_ref[...])
            return carry

        lax.fori_loop(0, rows, scores, 0, unroll=16)

        def expo(r, carry):
            q0 = pl.multiple_of(r * GRID_W, GRID_W)
            s = s_scr[pl.ds(q0, GRID_W), :]
            e_scr[pl.ds(q0, GRID_W), :] = jnp.exp(s - jnp.max(s, axis=-1, keepdims=True)).astype(BF16)
            return carry

        lax.fori_loop(0, rows, expo, 0, unroll=16)

        acc_scr[...] = _dot(e_scr[:, kw:], vcx_scr[...])

        def values(r, carry):
            row_start = jnp.clip(r - WIN_ROWS // 2, 0, rows - WIN_ROWS)
            q0 = pl.multiple_of(r * GRID_W, GRID_W)
            k0 = pl.multiple_of(row_start * GRID_W, GRID_W)
            t = acc_scr[pl.ds(q0, GRID_W), :] + _dot(e_scr[pl.ds(q0, GRID_W), :kw], vx_scr[pl.ds(k0, kw), :])
            o_ref[pl.ds(q0, GRID_W), lanes] = (t[:, :HEAD_DIM] / t[:, HEAD_DIM:]).astype(o_ref.dtype)
            return carry

        lax.fori_loop(0, rows, values, 0, unroll=16)


def _attention(qkv, ctx_kv, bias, cap, batch, seq, ctx_len, cast_weights=()):
    rows = seq // GRID_W
    kw = WIN_ROWS * GRID_W
    hps = ATTN_HEADS_PER_STEP
    n_hg = NA_HEADS // hps
    grid = (n_hg, batch)
    casts = [_Cast(cw, layer_, grid, col_tile) for cw, layer_, col_tile in cast_weights]
    return _pallas(
        _with_casts(functools.partial(_attn_kernel, rows=rows), 7, 1, casts),
        name="nbr_attention",
        grid=grid,
        in_specs=[
            pl.BlockSpec((seq, hps * HEAD_DIM), lambda h, b: (b, h)),
            pl.BlockSpec((seq, hps * HEAD_DIM), lambda h, b: (b, n_hg + h)),
            pl.BlockSpec((seq, hps * HEAD_DIM), lambda h, b: (b, 2 * n_hg + h)),
            pl.BlockSpec((ctx_len, hps * HEAD_DIM), lambda h, b: (b, h)),
            pl.BlockSpec((ctx_len, hps * HEAD_DIM), lambda h, b: (b, n_hg + h)),
            pl.BlockSpec((hps, WIN_ROWS, GRID_W, kw), lambda h, b: (h, 0, 0, 0)),
            pl.BlockSpec((GRID_W, kw), lambda h, b: (0, 0)),
        ] + [c.in_spec for c in casts],
        args=(qkv, qkv, qkv, ctx_kv, ctx_kv, bias, cap, *[c.w for c in casts]),
        out_specs=[pl.BlockSpec((seq, hps * HEAD_DIM), lambda h, b: (b, h))] + [c.out_spec for c in casts],
        out_shape=[jax.ShapeDtypeStruct((batch * seq, NA_WIDTH), BF16)] + [c.out_shape for c in casts],
        scratch_shapes=[
            pltpu.VMEM((seq, kw + ctx_len), F32),
            pltpu.VMEM((seq, kw + ctx_len), BF16),
            pltpu.VMEM((seq, 2 * HEAD_DIM), BF16),
            pltpu.VMEM((ctx_len, 2 * HEAD_DIM), BF16),
            pltpu.VMEM((seq, 2 * HEAD_DIM), F32),
        ],
        temp_bytes=_nbytes((seq, ctx_len), F32),
    )


def _residual_and_ffn_norm(rows, mod_row, x_ref, mix, gate_ref, gn_ref, shf_ref, scf_ref, x1_ref, h_ref):
    x1 = x_ref[rows, :] + gate_ref[pl.ds(mod_row, 1), :] * mix
    x1_ref[rows, :] = x1
    h = _norm_mod(x1, gn_ref[...], shf_ref[pl.ds(mod_row, 1), :], scf_ref[pl.ds(mod_row, 1), :])
    h_ref[rows, :] = h.astype(BF16)


def _row_chunks(tm):
    return [slice(r, r + ROW_CHUNK) for r in range(0, tm, ROW_CHUNK)]


def _skewed(chunks, matmul_fn, epilogue_fn):
    pending = None
    for rows in chunks:
        acc = matmul_fn(rows)
        if pending is not None:
            epilogue_fn(*pending)
        pending = (rows, acc)
    epilogue_fn(*pending)


def _even_out_kernel(attn_ref, b_ref, c_ref, u_ref, cp_ref, up_ref, cn_ref, un_ref, cw_ref, wo_ref,
                     x_ref, gate_ref, gn_ref, shf_ref, scf_ref, x1_ref, h_ref, conv_scr, *, tiles_per_mod, halo):
    i = pl.program_id(0)
    tm = c_ref.shape[0]
    t = i % tiles_per_mod
    conv_scr[:, :NA_WIDTH] = attn_ref[...]
    z = c_ref[...].astype(F32) * u_ref[...].astype(F32)
    z_before = cp_ref[halo - 1:halo, :].astype(F32) * up_ref[halo - 1:halo, :].astype(F32)
    z_after = cn_ref[0:1, :].astype(F32) * un_ref[0:1, :].astype(F32)
    z_before = jnp.where(t == 0, 0.0, z_before)
    z_after = jnp.where(t == tiles_per_mod - 1, 0.0, z_after)
    ridx = lax.broadcasted_iota(jnp.int32, (tm, 1), 0)
    z_prev = jnp.where(ridx == 0, z_before, pltpu.roll(z, 1, 0))
    z_next = jnp.where(ridx == tm - 1, z_after, pltpu.roll(z, tm - 1, 0))
    y = z_prev * cw_ref[0:1, :] + z * cw_ref[1:2, :] + z_next * cw_ref[2:3, :]
    conv_scr[:, NA_WIDTH:] = (b_ref[...].astype(F32) * y).astype(BF16)
    _skewed(
        _row_chunks(tm),
        lambda rows: _dot(conv_scr[rows, :], wo_ref[...]),
        lambda rows, mix: _residual_and_ffn_norm(rows, i // tiles_per_mod, x_ref, mix, gate_ref, gn_ref,
                                                 shf_ref, scf_ref, x1_ref, h_ref))


def _even_out(attn, proj, conv_w, w_out, x2d, mods, norm_ffn_g, layer, seq, tm):
    m = x2d.shape[0]
    halo = 16
    hb = tm // halo
    n_halo_blocks = m // halo
    cb = NA_WIDTH * 3 // CONV_CH
    kern = functools.partial(_even_out_kernel, tiles_per_mod=seq // tm, halo=halo)
    main = lambda c: pl.BlockSpec((tm, CONV_CH), lambda i: (i, c))
    prev = lambda c: pl.BlockSpec((halo, CONV_CH), lambda i: (jnp.maximum(i * hb - 1, 0), c))
    nxt = lambda c: pl.BlockSpec((halo, CONV_CH), lambda i: (jnp.minimum((i + 1) * hb, n_halo_blocks - 1), c))
    return _pallas(
        kern,
        name="even_out",
        grid=(m // tm,),
        in_specs=[
            pl.BlockSpec((tm, NA_WIDTH), lambda i: (i, 0)),
            main(cb), main(cb + 1), main(cb + 2),
            prev(cb + 1), prev(cb + 2), nxt(cb + 1), nxt(cb + 2),
            _resident_spec(conv_w.shape),
            _resident_spec(w_out.shape),
            pl.BlockSpec((tm, D_MODEL), lambda i: (i, 0)),
            _mod_spec(layer, 2, 1),
            pl.BlockSpec((None, 1, D_MODEL), lambda i: (layer, 0, 0)),
            _mod_spec(layer, 3, 1),
            _mod_spec(layer, 4, 1),
        ],
        args=(attn, proj, proj, proj, proj, proj, proj, proj, conv_w, w_out, x2d, mods, norm_ffn_g, mods, mods),
        out_specs=[
            pl.BlockSpec((tm, D_MODEL), lambda i: (i, 0)),
            pl.BlockSpec((tm, D_MODEL), lambda i: (i, 0)),
        ],
        out_shape=[
            jax.ShapeDtypeStruct((m, D_MODEL), F32),
            jax.ShapeDtypeStruct((m, D_MODEL), BF16),
        ],
        scratch_shapes=[pltpu.VMEM((tm, NA_WIDTH + CONV_CH), BF16)],
        temp_bytes=3 * _nbytes((tm, CONV_CH), F32) + 2 * _nbytes((tm, D_MODEL), F32),
    )


def _ffn_kernel(*refs, n_row_tiles, tiles_per_mod, modulated):
    if modulated:
        (h_ref, wg_ref, wu_ref, wd_ref, x_ref, gate_ref, gn_ref, sh_ref, sc_ref,
         x2_ref, y_ref, acc_ref) = refs
    else:
        h_ref, wg_ref, wu_ref, wd_ref, x_ref, gate_ref, gn_ref, y_ref, acc_ref = refs
    i = pl.program_id(0)
    f = pl.program_id(1)
    slot = i % 2
    tm = h_ref.shape[0]
    rs = tm // FFN_EPILOGUE_SLICES

    @pl.when((f == 0) & (i < n_row_tiles))
    def _():
        acc_ref[slot] = jnp.zeros((tm, D_MODEL), F32)

    def matmuls():
        rc = tm // FFN_ROW_CHUNKS
        ts = []
        for c in range(FFN_ROW_CHUNKS):
            h = h_ref[c * rc:(c + 1) * rc, :]
            ts.append((jax.nn.silu(_dot(h, wg_ref[...])) * _dot(h, wu_ref[...])).astype(BF16))
        for c in range(FFN_ROW_CHUNKS):
            acc_ref[slot, c * rc:(c + 1) * rc, :] += _dot(ts[c], wd_ref[...])

    def epilogue_slice():
        r0 = pl.multiple_of(f * rs, rs)
        row = (i - 1) // tiles_per_mod
        x2 = x_ref[...] + gate_ref[pl.ds(row, 1), :] * acc_ref[1 - slot, pl.ds(r0, rs), :]
        if modulated:
            x2_ref[...] = x2
            y = _norm_mod(x2, gn_ref[...], sh_ref[pl.ds(row, 1), :], sc_ref[pl.ds(row, 1), :])
        else:
            y = x2 * lax.rsqrt(jnp.mean(x2 * x2, axis=-1, keepdims=True) + EPS) * gn_ref[...]
        y_ref[...] = y.astype(y_ref.dtype)

    has_slice = (i > 0) & (f < FFN_EPILOGUE_SLICES)
    has_matmuls = i < n_row_tiles

    @pl.when(has_matmuls & jnp.logical_not(has_slice))
    def _():
        matmuls()

    @pl.when(has_matmuls & has_slice)
    def _():
        epilogue_slice()
        matmuls()

    @pl.when(jnp.logical_not(has_matmuls) & has_slice)
    def _():
        epilogue_slice()


def _ffn(h, w_gate, w_up, w_down, x2d, mods, layer, next_g, next_layer, seq, tm):
    m = x2d.shape[0]
    n_f, _, tf = w_gate.shape
    n_i = m // tm
    n_s = FFN_EPILOGUE_SLICES
    assert n_f >= n_s and (tm // n_s) % BF16_SUBLANES == 0
    modulated = next_layer is not None
    kern = functools.partial(_ffn_kernel, n_row_tiles=n_i, tiles_per_mod=seq // tm, modulated=modulated)
    w_tile = lambda i, f: jnp.where(i == n_i, n_f - 1, f)
    e_block = lambda i, f: jnp.where(i == 0, 0, (i - 1) * n_s + jnp.minimum(f, n_s - 1))
    slice_spec = pl.BlockSpec((tm // n_s, D_MODEL), lambda i, f: (e_block(i, f), 0))
    in_specs = [
        pl.BlockSpec((tm, D_MODEL), lambda i, f: (jnp.minimum(i, n_i - 1), 0)),
        pl.BlockSpec((None, D_MODEL, tf), lambda i, f: (w_tile(i, f), 0, 0)),
        pl.BlockSpec((None, D_MODEL, tf), lambda i, f: (w_tile(i, f), 0, 0)),
        pl.BlockSpec((tf, D_MODEL), lambda i, f: (w_tile(i, f), 0)),
        slice_spec,
        _mod_spec(layer, 5, 2),
    ]
    args = [h, w_gate, w_up, w_down, x2d, mods]
    if modulated:
        in_specs += [pl.BlockSpec((None, 1, D_MODEL), lambda i, f: (next_layer, 0, 0)),
                     _mod_spec(next_layer, 0, 2), _mod_spec(next_layer, 1, 2)]
        args += [next_g, mods, mods]
        out_specs = [slice_spec, slice_spec]
        out_shape = [jax.ShapeDtypeStruct((m, D_MODEL), F32), jax.ShapeDtypeStruct((m, D_MODEL), BF16)]
    else:
        in_specs += [pl.BlockSpec((1, D_MODEL), lambda i, f: (0, 0))]
        args += [next_g]
        out_specs = slice_spec
        out_shape = jax.ShapeDtypeStruct((m, D_MODEL), F32)
    rc = tm // FFN_ROW_CHUNKS
    return _pallas(
        kern,
        name="ffn_final" if not modulated else "ffn",
        grid=(n_i + 1, n_f),
        in_specs=in_specs,
        args=args,
        out_specs=out_specs,
        out_shape=out_shape,
        scratch_shapes=[pltpu.VMEM((2, tm, D_MODEL), F32)],
        temp_bytes=FFN_ROW_CHUNKS * (2 * _nbytes((rc, tf), F32) + _nbytes((rc, tf), BF16))
        + _nbytes((rc, D_MODEL), F32),
    )


def _odd_in_kernel(h_ref, w_ref, lg_ref, lb_ref, u_ref, v_ref):
    def matmuls(rows):
        h = h_ref[rows, :]
        return _dot(h, w_ref[:, :D_MODEL]), _dot(h, w_ref[:, D_MODEL:])

    def epilogue(rows, acc):
        u_ref[rows, :] = jax.nn.gelu(acc[0], approximate=True).astype(BF16)
        z = jax.nn.gelu(acc[1], approximate=True)
        zc = z - jnp.mean(z, axis=-1, keepdims=True)
        var = jnp.mean(zc * zc, axis=-1, keepdims=True)
        v_ref[rows, :] = (zc * lax.rsqrt(var + EPS) * lg_ref[...] + lb_ref[...]).astype(BF16)

    _skewed(_row_chunks(h_ref.shape[0]), matmuls, epilogue)


def _odd_in(h, w_in, ln_g, ln_b, tm):
    m = h.shape[0]
    row_spec = pl.BlockSpec((tm, D_MODEL), lambda i: (i, 0))
    return _pallas(
        _odd_in_kernel,
        name="odd_in",
        grid=(m // tm,),
        in_specs=[
            row_spec,
            _resident_spec(w_in.shape),
            pl.BlockSpec((1, D_MODEL), lambda i: (0, 0)),
            pl.BlockSpec((1, D_MODEL), lambda i: (0, 0)),
        ],
        args=(h, w_in, ln_g, ln_b),
        out_specs=[row_spec, row_spec],
        out_shape=[jax.ShapeDtypeStruct((m, D_MODEL), BF16)] * 2,
        temp_bytes=2 * 4 * _nbytes((ROW_CHUNK, D_MODEL), F32),
    )


def _odd_out_kernel(u_ref, v_ref, ws_ref, bs_ref, wo_ref, x_ref, gate_ref, gn_ref, shf_ref, scf_ref,
                    x1_ref, h_ref, t_scr, *, tiles_per_mod):
    i = pl.program_id(0)
    n_chunks = u_ref.shape[0] // CHUNK
    for g in range(SG_GROUPS):
        cols = slice(g * CHUNK, (g + 1) * CHUNK)
        vcat = jnp.concatenate([v_ref[n * CHUNK:(n + 1) * CHUNK, cols] for n in range(n_chunks)], axis=1)
        mixed = _dot(ws_ref[g], vcat)
        for n in range(n_chunks):
            rows = slice(n * CHUNK, (n + 1) * CHUNK)
            gated = u_ref[rows, cols].astype(F32) * (mixed[:, n * CHUNK:(n + 1) * CHUNK] + bs_ref[g])
            t_scr[rows, cols] = gated.astype(BF16)
    _skewed(
        _row_chunks(u_ref.shape[0]),
        lambda rows: _dot(t_scr[rows, :], wo_ref[...]),
        lambda rows, mix: _residual_and_ffn_norm(rows, i // tiles_per_mod, x_ref, mix, gate_ref, gn_ref,
                                                 shf_ref, scf_ref, x1_ref, h_ref))


def _odd_out(u, v, w_s, b_s, w_out, x2d, mods, norm_ffn_g, layer, seq, tm):
    m = x2d.shape[0]
    kern = functools.partial(_odd_out_kernel, tiles_per_mod=seq // tm)
    return _pallas(
        kern,
        name="odd_out",
        grid=(m // tm,),
        in_specs=[
            pl.BlockSpec((tm, D_MODEL), lambda i: (i, 0)),
            pl.BlockSpec((tm, D_MODEL), lambda i: (i, 0)),
            _resident_spec(w_s.shape),
            _resident_spec(b_s.shape),
            _resident_spec(w_out.shape),
            pl.BlockSpec((tm, D_MODEL), lambda i: (i, 0)),
            _mod_spec(layer, 2, 1),
            pl.BlockSpec((None, 1, D_MODEL), lambda i: (layer, 0, 0)),
            _mod_spec(layer, 3, 1),
            _mod_spec(layer, 4, 1),
        ],
        args=(u, v, w_s, b_s, w_out, x2d, mods, norm_ffn_g, mods, mods),
        out_specs=[
            pl.BlockSpec((tm, D_MODEL), lambda i: (i, 0)),
            pl.BlockSpec((tm, D_MODEL), lambda i: (i, 0)),
        ],
        out_shape=[
            jax.ShapeDtypeStruct((m, D_MODEL), F32),
            jax.ShapeDtypeStruct((m, D_MODEL), BF16),
        ],
        scratch_shapes=[pltpu.VMEM((tm, D_MODEL), BF16)],
        temp_bytes=2 * 2 * _nbytes((ROW_CHUNK, D_MODEL), F32),
    )


def kernel(x, c, ctx, c_ctx, ada_w, ada_b, norm_mix_g, norm_ffn_g, ffn_w_gate, ffn_w_up, ffn_w_down,
           even_w_in, even_w_out, rpb, conv_w, odd_w_in, odd_w_out, sg_ln_g, sg_ln_b, sg_w, sg_b,
           final_norm_g):
    batch, seq, d = x.shape
    ctx_len = ctx.shape[1]
    depth = ada_w.shape[0]
    assert d == D_MODEL and depth == 2 and batch < MOD_ROWS and seq % (GRID_W * WIN_ROWS) == 0

    bf = lambda w: w.astype(BF16)
    x2d = x.reshape(batch * seq, d)
    ctx2d = ctx.reshape(batch * ctx_len, d)
    g_mix = norm_mix_g.reshape(depth, 1, d)
    g_ffn = norm_ffn_g.reshape(depth, 1, d)

    cond = jnp.concatenate([c, c_ctx[None], jnp.zeros((MOD_ROWS - batch - 1, d), F32)], axis=0)
    mods = _ada_mods(cond, ada_w, ada_b)

    ffn_casts = lambda layer: [(ffn_w_gate, layer, FFN_TILE), (ffn_w_up, layer, FFN_TILE),
                               (ffn_w_down, layer, None)]
    w_in0 = bf(even_w_in[0])
    proj, w_out0, wg0, wu0, wd0 = _norm_matmul(
        x2d, g_mix, mods, 0, w_in0, col0=0, n_cols=w_in0.shape[1], rows_per_mod=seq, mod_row0=0,
        tm=1024, tn=1536, name="even_in", cast_weights=[(even_w_out, 0, None)] + ffn_casts(0))
    ctx_kv, = _norm_matmul(ctx2d, g_mix, mods, 0, w_in0, col0=NA_WIDTH, n_cols=2 * NA_WIDTH,
                           rows_per_mod=batch * ctx_len, mod_row0=batch, tm=1024, tn=1024, name="ctx_kv")
    bias, cap = _bias_tables(rpb[0])
    attn, w_in1, w_out1, wg1, wu1, wd1 = _attention(
        proj, ctx_kv, bias, cap, batch, seq, ctx_len,
        cast_weights=[(odd_w_in, 0, None), (odd_w_out, 0, None)] + ffn_casts(1))
    x1, h = _even_out(attn, proj, conv_w[0], w_out0, x2d, mods, g_ffn, 0, seq, tm=512)
    x2, h = _ffn(h, wg0, wu0, wd0, x1, mods, 0, g_mix, 1, seq, tm=1024)

    u, v = _odd_in(h, w_in1, sg_ln_g[0][None], sg_ln_b[0][None], tm=512)
    b_s = jnp.broadcast_to(sg_b[0][:, :, None], (SG_GROUPS, CHUNK, CHUNK))
    x1, h = _odd_out(u, v, bf(sg_w[0]), b_s, w_out1, x2, mods, g_ffn, 1, seq, tm=512)
    out = _ffn(h, wg1, wu1, wd1, x1, mods, 1, final_norm_g[None], None, seq, tm=1024)
    return out.reshape(batch, seq, d)
```

```python
import functools

import jax
import jax.numpy as jnp
from jax import lax
from jax.experimental import pallas as pl
from jax.experimental.pallas import tpu as pltpu

D_MODEL = 2048
GRID_W = 64
NA_HEADS = 8
HEAD_DIM = 128
NA_WIDTH = NA_HEADS * HEAD_DIM
CONV_CH = D_MODEL - NA_WIDTH
WIN_ROWS = 8
WIN_COLS = 16
CHUNK = 128
SG_GROUPS = 16
EPS = 1e-6
NEG_INF = -1e30
UNMASKED_CAP = 3e38
MOD_ROWS = 16
ATTN_HEADS_PER_STEP = 2
NORM_SLICES = 4
FFN_TILE = 512
FFN_ROW_CHUNKS = 4
FFN_EPILOGUE_SLICES = 8
ROW_CHUNK = 256

F32 = jnp.float32
BF16 = jnp.bfloat16
BF16_SUBLANES = 16

MIB = 1024 * 1024
V7X_VMEM_BYTES = 64 * MIB
VMEM_CAP_BYTES = V7X_VMEM_BYTES - 5 * MIB
VMEM_SLACK_BYTES = 2 * MIB


def _nbytes(shape, dtype):
    n = jnp.dtype(dtype).itemsize
    for d in shape:
        n *= 1 if d is None else d
    return n


def _pallas(kernel_fn, *, name, grid, in_specs, args, out_specs, out_shape, scratch_shapes=(), temp_bytes):
    outs = list(zip(out_specs, out_shape)) if isinstance(out_specs, (list, tuple)) else [(out_specs, out_shape)]
    total = temp_bytes + VMEM_SLACK_BYTES
    for spec, arg in zip(in_specs, args):
        if spec.block_shape is not None:
            n_buffers = 2 if spec.pipeline_mode is None else spec.pipeline_mode.buffer_count
            total += n_buffers * _nbytes(spec.block_shape, arg.dtype)
    for spec, shape in outs:
        total += 2 * _nbytes(spec.block_shape, shape.dtype)
    for s in scratch_shapes:
        total += _nbytes(s.shape, s.dtype)
    assert total <= VMEM_CAP_BYTES, (name, total)
    return pl.pallas_call(
        kernel_fn, grid=grid, in_specs=in_specs, out_specs=out_specs, out_shape=out_shape,
        scratch_shapes=list(scratch_shapes),
        compiler_params=pltpu.CompilerParams(dimension_semantics=("arbitrary",) * len(grid),
                                             vmem_limit_bytes=total),
        name=name,
    )(*args)


def _resident_spec(shape):
    zeros = (0,) * len(shape)
    return pl.BlockSpec(shape, lambda *_: zeros, pipeline_mode=pl.Buffered(1))


def _dot(a, b):
    return jnp.dot(a, b, preferred_element_type=F32)


def _norm_mod(x, g, shift, scale):
    y = x * lax.rsqrt(jnp.mean(x * x, axis=-1, keepdims=True) + EPS)
    return (y * g) * (1.0 + scale) + shift


def _linear_step(ids, grid):
    step = ids[0]
    for axis in range(1, len(grid)):
        step = step * grid[axis] + ids[axis]
    return step


class _Cast:
    def __init__(self, stacked_w, layer, grid, col_tile=None):
        _, rows, cols = stacked_w.shape
        n_steps = 1
        for g in grid:
            n_steps *= g
        self.w = stacked_w
        self.grid = grid
        self.col_tile = col_tile
        self.n_blocks = nb = max(n for n in range(1, n_steps + 1)
                                 if rows % n == 0 and (rows // n) % BF16_SUBLANES == 0)
        rb = rows // nb
        block = lambda *ids: jnp.minimum(_linear_step(ids, grid), nb - 1)
        self.in_spec = pl.BlockSpec((None, rb, cols), lambda *ids: (layer, block(*ids), 0))
        if col_tile is None:
            self.out_shape = jax.ShapeDtypeStruct((rows, cols), BF16)
            self.out_spec = pl.BlockSpec((rb, cols), lambda *ids: (block(*ids), 0))
        else:
            nt = cols // col_tile
            self.out_shape = jax.ShapeDtypeStruct((nt, rows, col_tile), BF16)
            self.out_spec = pl.BlockSpec((nt, rb, col_tile), lambda *ids: (0, block(*ids), 0))

    def run(self, w_ref, o_ref):
        step = _linear_step([pl.program_id(a) for a in range(len(self.grid))], self.grid)

        @pl.when(step < self.n_blocks)
        def _():
            if self.col_tile is None:
                o_ref[...] = w_ref[...].astype(BF16)
            else:
                for t in range(o_ref.shape[0]):
                    o_ref[t] = w_ref[:, t * self.col_tile:(t + 1) * self.col_tile].astype(BF16)


def _with_casts(body, n_in, n_out, casts):
    n_c = len(casts)

    def kern(*refs):
        ins, rest = refs[:n_in], refs[n_in:]
        cast_ins, rest = rest[:n_c], rest[n_c:]
        outs, rest = rest[:n_out], rest[n_out:]
        cast_outs, scratch = rest[:n_c], rest[n_c:]
        body(*ins, *outs, *scratch)
        for cast, w_ref, o_ref in zip(casts, cast_ins, cast_outs):
            cast.run(w_ref, o_ref)

    return kern


def _ada_kernel(cond_ref, w_ref, b_ref, o_ref):
    a = jax.nn.silu(cond_ref[...]).astype(BF16)
    o_ref[...] = _dot(a, w_ref[...].astype(BF16)) + b_ref[...]


def _ada_mods(cond, ada_w, ada_b):
    depth, _, n = ada_w.shape
    tn = 1024
    return _pallas(
        _ada_kernel,
        name="ada_mods",
        grid=(depth, n // tn),
        in_specs=[
            pl.BlockSpec((MOD_ROWS, D_MODEL), lambda l, j: (0, 0)),
            pl.BlockSpec((None, D_MODEL, tn), lambda l, j: (l, 0, j)),
            pl.BlockSpec((None, 1, tn), lambda l, j: (l, 0, j)),
        ],
        args=(cond, ada_w, ada_b.reshape(depth, 1, n)),
        out_specs=pl.BlockSpec((None, MOD_ROWS, tn), lambda l, j: (l, 0, j)),
        out_shape=jax.ShapeDtypeStruct((depth, MOD_ROWS, n), F32),
        temp_bytes=_nbytes((D_MODEL, tn), BF16),
    )


def _mod_spec(layer, chunk, n_grid_axes):
    if n_grid_axes == 1:
        return pl.BlockSpec((None, MOD_ROWS, D_MODEL), lambda i: (layer, 0, chunk))
    return pl.BlockSpec((None, MOD_ROWS, D_MODEL), lambda i, j: (layer, 0, chunk))


def _norm_matmul_kernel(x_ref, g_ref, sh_ref, sc_ref, w_ref, o_ref, h_scr, *,
                        n_row_tiles, n_slices, tiles_per_mod, mod_row0):
    i = pl.program_id(0)
    j = pl.program_id(1)
    rs = x_ref.shape[0]

    def norm_slice():
        r0 = pl.multiple_of(jnp.minimum(j, n_slices - 1) * rs, rs)
        row = mod_row0 + i // tiles_per_mod
        h = _norm_mod(x_ref[...], g_ref[...], sh_ref[pl.ds(row, 1), :], sc_ref[pl.ds(row, 1), :])
        h_scr[i % 2, pl.ds(r0, rs), :] = h.astype(BF16)

    def matmul():
        o_ref[...] = _dot(h_scr[(i - 1) % 2], w_ref[...]).astype(o_ref.dtype)

    @pl.when(i == 0)
    def _():
        norm_slice()

    @pl.when((i > 0) & (i < n_row_tiles))
    def _():
        matmul()
        norm_slice()

    @pl.when(i == n_row_tiles)
    def _():
        matmul()


def _norm_matmul(x2d, g, mods, layer, w, *, col0, n_cols, rows_per_mod, mod_row0, tm, tn, name,
                 cast_weights=()):
    m = x2d.shape[0]
    n_i, n_j = m // tm, n_cols // tn
    n_s = min(n_j, NORM_SLICES)
    grid = (n_i + 1, n_j)
    body = functools.partial(_norm_matmul_kernel, n_row_tiles=n_i, n_slices=n_s,
                             tiles_per_mod=max(rows_per_mod // tm, 1), mod_row0=mod_row0)
    casts = [_Cast(cw, layer_, grid, col_tile) for cw, layer_, col_tile in cast_weights]
    cb0 = col0 // tn
    x_block = lambda i, j: jnp.minimum(i, n_i - 1) * n_s + jnp.minimum(j, n_s - 1)
    w_block = lambda i, j: cb0 + jnp.where(i == 0, 0, j)
    return _pallas(
        _with_casts(body, 5, 1, casts),
        name=name,
        grid=grid,
        in_specs=[
            pl.BlockSpec((tm // n_s, D_MODEL), lambda i, j: (x_block(i, j), 0)),
            pl.BlockSpec((None, 1, D_MODEL), lambda i, j: (layer, 0, 0)),
            _mod_spec(layer, 0, 2),
            _mod_spec(layer, 1, 2),
            pl.BlockSpec((D_MODEL, tn), lambda i, j: (0, w_block(i, j))),
        ] + [c.in_spec for c in casts],
        args=(x2d, g, mods, mods, w, *[c.w for c in casts]),
        out_specs=[pl.BlockSpec((tm, tn), lambda i, j: (jnp.maximum(i - 1, 0), jnp.where(i == 0, 0, j)))]
        + [c.out_spec for c in casts],
        out_shape=[jax.ShapeDtypeStruct((m, n_cols), BF16)] + [c.out_shape for c in casts],
        scratch_shapes=[pltpu.VMEM((2, tm, D_MODEL), BF16)],
        temp_bytes=_nbytes((tm, tn), F32) + 3 * _nbytes((tm // n_s, D_MODEL), F32),
    )


def _bias_kernel(rpb_ref, bias_ref, cap_ref):
    h = pl.program_id(0)
    n_dc = 2 * WIN_COLS - 1
    n_dr = 2 * WIN_ROWS - 1
    q = lax.broadcasted_iota(jnp.int32, (GRID_W, 2 * GRID_W), 0)
    lane = lax.broadcasted_iota(jnp.int32, (GRID_W, 2 * GRID_W), 1)
    second = lane >= GRID_W
    k = jnp.where(second, lane - GRID_W, lane)
    dc = jnp.clip(k - q, -(WIN_COLS - 1), WIN_COLS - 1) + (WIN_COLS - 1)
    pairs = []
    for m in range(n_dr - 1):
        acc = jnp.zeros((GRID_W, 2 * GRID_W), F32)
        for d in range(n_dc):
            val = jnp.where(second, rpb_ref[h, (m + 1) * n_dc + d], rpb_ref[h, m * n_dc + d])
            acc = jnp.where(dc == d, val, acc)
        pairs.append(acc)
    for o in range(WIN_ROWS):
        for jj in range(WIN_ROWS // 2):
            bias_ref[o, :, jj * 2 * GRID_W:(jj + 1) * 2 * GRID_W] = pairs[o + 2 * jj]
    col_start = jnp.clip(q - WIN_COLS // 2, 0, GRID_W - WIN_COLS)
    cap = jnp.where((k >= col_start) & (k < col_start + WIN_COLS), UNMASKED_CAP, NEG_INF).astype(F32)
    for jj in range(WIN_ROWS // 2):
        cap_ref[:, jj * 2 * GRID_W:(jj + 1) * 2 * GRID_W] = cap


def _bias_tables(rpb):
    h = rpb.shape[0]
    kw = WIN_ROWS * GRID_W
    return _pallas(
        _bias_kernel,
        name="bias_tables",
        grid=(h,),
        in_specs=[pl.BlockSpec(memory_space=pltpu.SMEM)],
        args=(rpb.reshape(h, -1),),
        out_specs=[
            pl.BlockSpec((None, WIN_ROWS, GRID_W, kw), lambda i: (i, 0, 0, 0)),
            pl.BlockSpec((GRID_W, kw), lambda i: (0, 0)),
        ],
        out_shape=[
            jax.ShapeDtypeStruct((h, WIN_ROWS, GRID_W, kw), F32),
            jax.ShapeDtypeStruct((GRID_W, kw), F32),
        ],
        temp_bytes=2 * WIN_ROWS * _nbytes((GRID_W, 2 * GRID_W), F32),
    )


def _attn_kernel(q_ref, k_ref, v_ref, kc_ref, vc_ref, bias_ref, cap_ref, o_ref, *scratch, rows):
    scale = HEAD_DIM ** -0.5
    kw = WIN_ROWS * GRID_W
    ctx_len = kc_ref.shape[0]
    nt = (((1,), (1,)), ((), ()))
    per_head = len(scratch) // ATTN_HEADS_PER_STEP

    def parts(head):
        return scratch[head * per_head:(head + 1) * per_head], slice(head * HEAD_DIM, (head + 1) * HEAD_DIM)

    def setup(head):
        (s_scr, _, vx_scr, vcx_scr, _), lanes = parts(head)
        vx_scr[:, :HEAD_DIM] = v_ref[:, lanes]
        vx_scr[:, HEAD_DIM:] = jnp.ones((v_ref.shape[0], HEAD_DIM), BF16)
        vcx_scr[:, :HEAD_DIM] = vc_ref[:, lanes]
        vcx_scr[:, HEAD_DIM:] = jnp.ones((ctx_len, HEAD_DIM), BF16)
        s_scr[:, kw:] = lax.dot_general(q_ref[:, lanes], kc_ref[:, lanes], nt,
                                        preferred_element_type=F32) * scale

    def scores(head, r):
        (s_scr, _, _, _, _), lanes = parts(head)
        row_start = jnp.clip(r - WIN_ROWS // 2, 0, rows - WIN_ROWS)
        o = row_start - r + (WIN_ROWS - 1)
        q0 = pl.multiple_of(r * GRID_W, GRID_W)
        k0 = pl.multiple_of(row_start * GRID_W, GRID_W)
        s = lax.dot_general(q_ref[pl.ds(q0, GRID_W), lanes], k_ref[pl.ds(k0, kw), lanes], nt,
                            preferred_element_type=F32)
        s_scr[pl.ds(q0, GRID_W), :kw] = jnp.minimum(s * scale + bias_ref[head, o], cap_ref[...])

    def expo(head, r):
        (s_scr, e_scr, _, _, _), _ = parts(head)
        q0 = pl.multiple_of(r * GRID_W, GRID_W)
        s = s_scr[pl.ds(q0, GRID_W), :]
        e_scr[pl.ds(q0, GRID_W), :] = jnp.exp(s - jnp.max(s, axis=-1, keepdims=True)).astype(BF16)

    def ctx_values(head):
        (_, e_scr, _, vcx_scr, acc_scr), _ = parts(head)
        acc_scr[...] = _dot(e_scr[:, kw:], vcx_scr[...])

    def values(head, r):
        (_, e_scr, vx_scr, _, acc_scr), lanes = parts(head)
        row_start = jnp.clip(r - WIN_ROWS // 2, 0, rows - WIN_ROWS)
        q0 = pl.multiple_of(r * GRID_W, GRID_W)
        k0 = pl.multiple_of(row_start * GRID_W, GRID_W)
        t = acc_scr[pl.ds(q0, GRID_W), :] + _dot(e_scr[pl.ds(q0, GRID_W), :kw], vx_scr[pl.ds(k0, kw), :])
        o_ref[pl.ds(q0, GRID_W), lanes] = (t[:, :HEAD_DIM] / t[:, HEAD_DIM:]).astype(o_ref.dtype)

    def loop(*row_fns):
        def body(r, carry):
            for fn in row_fns:
                fn(r)
            return carry
        lax.fori_loop(0, rows, body, 0, unroll=16)

    setup(0)
    setup(1)
    loop(functools.partial(scores, 0))
    loop(functools.partial(expo, 0), functools.partial(scores, 1))
    ctx_values(0)
    loop(functools.partial(expo, 1), functools.partial(values, 0))
    ctx_values(1)
    loop(functools.partial(values, 1))


def _attention(qkv, ctx_kv, bias, cap, batch, seq, ctx_len, cast_weights=()):
    rows = seq // GRID_W
    kw = WIN_ROWS * GRID_W
    hps = ATTN_HEADS_PER_STEP
    n_hg = NA_HEADS // hps
    grid = (n_hg, batch)
    casts = [_Cast(cw, layer_, grid, col_tile) for cw, layer_, col_tile in cast_weights]
    return _pallas(
        _with_casts(functools.partial(_attn_kernel, rows=rows), 7, 1, casts),
        name="nbr_attention",
        grid=grid,
        in_specs=[
            pl.BlockSpec((seq, hps * HEAD_DIM), lambda h, b: (b, h)),
            pl.BlockSpec((seq, hps * HEAD_DIM), lambda h, b: (b, n_hg + h)),
            pl.BlockSpec((seq, hps * HEAD_DIM), lambda h, b: (b, 2 * n_hg + h)),
            pl.BlockSpec((ctx_len, hps * HEAD_DIM), lambda h, b: (b, h)),
            pl.BlockSpec((ctx_len, hps * HEAD_DIM), lambda h, b: (b, n_hg + h)),
            pl.BlockSpec((hps, WIN_ROWS, GRID_W, kw), lambda h, b: (h, 0, 0, 0)),
            pl.BlockSpec((GRID_W, kw), lambda h, b: (0, 0)),
        ] + [c.in_spec for c in casts],
        args=(qkv, qkv, qkv, ctx_kv, ctx_kv, bias, cap, *[c.w for c in casts]),
        out_specs=[pl.BlockSpec((seq, hps * HEAD_DIM), lambda h, b: (b, h))] + [c.out_spec for c in casts],
        out_shape=[jax.ShapeDtypeStruct((batch * seq, NA_WIDTH), BF16)] + [c.out_shape for c in casts],
        scratch_shapes=[
            pltpu.VMEM((seq, kw + ctx_len), F32),
            pltpu.VMEM((seq, kw + ctx_len), BF16),
            pltpu.VMEM((seq, 2 * HEAD_DIM), BF16),
            pltpu.VMEM((ctx_len, 2 * HEAD_DIM), BF16),
            pltpu.VMEM((seq, 2 * HEAD_DIM), F32),
        ] * hps,
        temp_bytes=_nbytes((seq, ctx_len), F32),
    )


def _residual_and_ffn_norm(rows, mod_row, x_ref, mix, gate_ref, gn_ref, shf_ref, scf_ref, x1_ref, h_ref):
    x1 = x_ref[rows, :] + gate_ref[pl.ds(mod_row, 1), :] * mix
    x1_ref[rows, :] = x1
    h = _norm_mod(x1, gn_ref[...], shf_ref[pl.ds(mod_row, 1), :], scf_ref[pl.ds(mod_row, 1), :])
    h_ref[rows, :] = h.astype(BF16)


def _row_chunks(tm):
    return [slice(r, r + ROW_CHUNK) for r in range(0, tm, ROW_CHUNK)]


def _skewed(chunks, matmul_fn, epilogue_fn):
    pending = None
    for rows in chunks:
        acc = matmul_fn(rows)
        if pending is not None:
            epilogue_fn(*pending)
        pending = (rows, acc)
    epilogue_fn(*pending)


def _even_out_kernel(attn_ref, b_ref, c_ref, u_ref, cp_ref, up_ref, cn_ref, un_ref, cw_ref, wo_ref,
                     x_ref, gate_ref, gn_ref, shf_ref, scf_ref, x1_ref, h_ref, conv_scr, *, tiles_per_mod, halo):
    i = pl.program_id(0)
    tm = c_ref.shape[0]
    t = i % tiles_per_mod
    conv_scr[:, :NA_WIDTH] = attn_ref[...]
    z = c_ref[...].astype(F32) * u_ref[...].astype(F32)
    z_before = cp_ref[halo - 1:halo, :].astype(F32) * up_ref[halo - 1:halo, :].astype(F32)
    z_after = cn_ref[0:1, :].astype(F32) * un_ref[0:1, :].astype(F32)
    z_before = jnp.where(t == 0, 0.0, z_before)
    z_after = jnp.where(t == tiles_per_mod - 1, 0.0, z_after)
    ridx = lax.broadcasted_iota(jnp.int32, (tm, 1), 0)
    z_prev = jnp.where(ridx == 0, z_before, pltpu.roll(z, 1, 0))
    z_next = jnp.where(ridx == tm - 1, z_after, pltpu.roll(z, tm - 1, 0))
    y = z_prev * cw_ref[0:1, :] + z * cw_ref[1:2, :] + z_next * cw_ref[2:3, :]
    conv_scr[:, NA_WIDTH:] = (b_ref[...].astype(F32) * y).astype(BF16)
    _skewed(
        _row_chunks(tm),
        lambda rows: _dot(conv_scr[rows, :], wo_ref[...]),
        lambda rows, mix: _residual_and_ffn_norm(rows, i // tiles_per_mod, x_ref, mix, gate_ref, gn_ref,
                                                 shf_ref, scf_ref, x1_ref, h_ref))


def _even_out(attn, proj, conv_w, w_out, x2d, mods, norm_ffn_g, layer, seq, tm):
    m = x2d.shape[0]
    halo = 16
    hb = tm // halo
    n_halo_blocks = m // halo
    cb = NA_WIDTH * 3 // CONV_CH
    kern = functools.partial(_even_out_kernel, tiles_per_mod=seq // tm, halo=halo)
    main = lambda c: pl.BlockSpec((tm, CONV_CH), lambda i: (i, c))
    prev = lambda c: pl.BlockSpec((halo, CONV_CH), lambda i: (jnp.maximum(i * hb - 1, 0), c))
    nxt = lambda c: pl.BlockSpec((halo, CONV_CH), lambda i: (jnp.minimum((i + 1) * hb, n_halo_blocks - 1), c))
    return _pallas(
        kern,
        name="even_out",
        grid=(m // tm,),
        in_specs=[
            pl.BlockSpec((tm, NA_WIDTH), lambda i: (i, 0)),
            main(cb), main(cb + 1), main(cb + 2),
            prev(cb + 1), prev(cb + 2), nxt(cb + 1), nxt(cb + 2),
            _resident_spec(conv_w.shape),
            _resident_spec(w_out.shape),
            pl.BlockSpec((tm, D_MODEL), lambda i: (i, 0)),
            _mod_spec(layer, 2, 1),
            pl.BlockSpec((None, 1, D_MODEL), lambda i: (layer, 0, 0)),
            _mod_spec(layer, 3, 1),
            _mod_spec(layer, 4, 1),
        ],
        args=(attn, proj, proj, proj, proj, proj, proj, proj, conv_w, w_out, x2d, mods, norm_ffn_g, mods, mods),
        out_specs=[
            pl.BlockSpec((tm, D_MODEL), lambda i: (i, 0)),
            pl.BlockSpec((tm, D_MODEL), lambda i: (i, 0)),
        ],
        out_shape=[
            jax.ShapeDtypeStruct((m, D_MODEL), F32),
            jax.ShapeDtypeStruct((m, D_MODEL), BF16),
        ],
        scratch_shapes=[pltpu.VMEM((tm, NA_WIDTH + CONV_CH), BF16)],
        temp_bytes=3 * _nbytes((tm, CONV_CH), F32) + 2 * _nbytes((tm, D_MODEL), F32),
    )


def _ffn_kernel(*refs, n_row_tiles, tiles_per_mod, modulated):
    if modulated:
        (h_ref, wg_ref, wu_ref, wd_ref, x_ref, gate_ref, gn_ref, sh_ref, sc_ref,
         x2_ref, y_ref, acc_ref) = refs
    else:
        h_ref, wg_ref, wu_ref, wd_ref, x_ref, gate_ref, gn_ref, y_ref, acc_ref = refs
    i = pl.program_id(0)
    f = pl.program_id(1)
    slot = i % 2
    tm = h_ref.shape[0]
    rs = tm // FFN_EPILOGUE_SLICES

    @pl.when((f == 0) & (i < n_row_tiles))
    def _():
        acc_ref[slot] = jnp.zeros((tm, D_MODEL), F32)

    def matmuls():
        rc = tm // FFN_ROW_CHUNKS
        ts = []
        for c in range(FFN_ROW_CHUNKS):
            h = h_ref[c * rc:(c + 1) * rc, :]
            ts.append((jax.nn.silu(_dot(h, wg_ref[...])) * _dot(h, wu_ref[...])).astype(BF16))
        for c in range(FFN_ROW_CHUNKS):
            acc_ref[slot, c * rc:(c + 1) * rc, :] += _dot(ts[c], wd_ref[...])

    def epilogue_slice():
        r0 = pl.multiple_of(f * rs, rs)
        row = (i - 1) // tiles_per_mod
        x2 = x_ref[...] + gate_ref[pl.ds(row, 1), :] * acc_ref[1 - slot, pl.ds(r0, rs), :]
        if modulated:
            x2_ref[...] = x2
            y = _norm_mod(x2, gn_ref[...], sh_ref[pl.ds(row, 1), :], sc_ref[pl.ds(row, 1), :])
        else:
            y = x2 * lax.rsqrt(jnp.mean(x2 * x2, axis=-1, keepdims=True) + EPS) * gn_ref[...]
        y_ref[...] = y.astype(y_ref.dtype)

    has_slice = (i > 0) & (f < FFN_EPILOGUE_SLICES)
    has_matmuls = i < n_row_tiles

    @pl.when(has_matmuls & jnp.logical_not(has_slice))
    def _():
        matmuls()

    @pl.when(has_matmuls & has_slice)
    def _():
        epilogue_slice()
        matmuls()

    @pl.when(jnp.logical_not(has_matmuls) & has_slice)
    def _():
        epilogue_slice()


def _ffn(h, w_gate, w_up, w_down, x2d, mods, layer, next_g, next_layer, seq, tm):
    m = x2d.shape[0]
    n_f, _, tf = w_gate.shape
    n_i = m // tm
    n_s = FFN_EPILOGUE_SLICES
    assert n_f >= n_s and (tm // n_s) % BF16_SUBLANES == 0
    modulated = next_layer is not None
    kern = functools.partial(_ffn_kernel, n_row_tiles=n_i, tiles_per_mod=seq // tm, modulated=modulated)
    w_tile = lambda i, f: jnp.where(i == n_i, n_f - 1, f)
    e_block = lambda i, f: jnp.where(i == 0, 0, (i - 1) * n_s + jnp.minimum(f, n_s - 1))
    slice_spec = pl.BlockSpec((tm // n_s, D_MODEL), lambda i, f: (e_block(i, f), 0))
    in_specs = [
        pl.BlockSpec((tm, D_MODEL), lambda i, f: (jnp.minimum(i, n_i - 1), 0)),
        pl.BlockSpec((None, D_MODEL, tf), lambda i, f: (w_tile(i, f), 0, 0)),
        pl.BlockSpec((None, D_MODEL, tf), lambda i, f: (w_tile(i, f), 0, 0)),
        pl.BlockSpec((tf, D_MODEL), lambda i, f: (w_tile(i, f), 0)),
        slice_spec,
        _mod_spec(layer, 5, 2),
    ]
    args = [h, w_gate, w_up, w_down, x2d, mods]
    if modulated:
        in_specs += [pl.BlockSpec((None, 1, D_MODEL), lambda i, f: (next_layer, 0, 0)),
                     _mod_spec(next_layer, 0, 2), _mod_spec(next_layer, 1, 2)]
        args += [next_g, mods, mods]
        out_specs = [slice_spec, slice_spec]
        out_shape = [jax.ShapeDtypeStruct((m, D_MODEL), F32), jax.ShapeDtypeStruct((m, D_MODEL), BF16)]
    else:
        in_specs += [pl.BlockSpec((1, D_MODEL), lambda i, f: (0, 0))]
        args += [next_g]
        out_specs = slice_spec
        out_shape = jax.ShapeDtypeStruct((m, D_MODEL), F32)
    rc = tm // FFN_ROW_CHUNKS
    return _pallas(
        kern,
        name="ffn_final" if not modulated else "ffn",
        grid=(n_i + 1, n_f),
        in_specs=in_specs,
        args=args,
        out_specs=out_specs,
        out_shape=out_shape,
        scratch_shapes=[pltpu.VMEM((2, tm, D_MODEL), F32)],
        temp_bytes=FFN_ROW_CHUNKS * (2 * _nbytes((rc, tf), F32) + _nbytes((rc, tf), BF16))
        + _nbytes((rc, D_MODEL), F32),
    )


def _odd_in_kernel(h_ref, w_ref, lg_ref, lb_ref, u_ref, v_ref):
    def matmuls(rows):
        h = h_ref[rows, :]
        return _dot(h, w_ref[:, :D_MODEL]), _dot(h, w_ref[:, D_MODEL:])

    def epilogue(rows, acc):
        u_ref[rows, :] = jax.nn.gelu(acc[0], approximate=True).astype(BF16)
        z = jax.nn.gelu(acc[1], approximate=True)
        zc = z - jnp.mean(z, axis=-1, keepdims=True)
        var = jnp.mean(zc * zc, axis=-1, keepdims=True)
        v_ref[rows, :] = (zc * lax.rsqrt(var + EPS) * lg_ref[...] + lb_ref[...]).astype(BF16)

    _skewed(_row_chunks(h_ref.shape[0]), matmuls, epilogue)


def _odd_in(h, w_in, ln_g, ln_b, tm):
    m = h.shape[0]
    row_spec = pl.BlockSpec((tm, D_MODEL), lambda i: (i, 0))
    return _pallas(
        _odd_in_kernel,
        name="odd_in",
        grid=(m // tm,),
        in_specs=[
            row_spec,
            _resident_spec(w_in.shape),
            pl.BlockSpec((1, D_MODEL), lambda i: (0, 0)),
            pl.BlockSpec((1, D_MODEL), lambda i: (0, 0)),
        ],
        args=(h, w_in, ln_g, ln_b),
        out_specs=[row_spec, row_spec],
        out_shape=[jax.ShapeDtypeStruct((m, D_MODEL), BF16)] * 2,
        temp_bytes=2 * 4 * _nbytes((ROW_CHUNK, D_MODEL), F32),
    )


def _odd_out_kernel(u_ref, v_ref, ws_ref, bs_ref, wo_ref, x_ref, gate_ref, gn_ref, shf_ref, scf_ref,
                    x1_ref, h_ref, t_scr, *, tiles_per_mod):
    i = pl.program_id(0)
    n_chunks = u_ref.shape[0] // CHUNK
    for g in range(SG_GROUPS):
        cols = slice(g * CHUNK, (g + 1) * CHUNK)
        vcat = jnp.concatenate([v_ref[n * CHUNK:(n + 1) * CHUNK, cols] for n in range(n_chunks)], axis=1)
        mixed = _dot(ws_ref[g], vcat)
        for n in range(n_chunks):
            rows = slice(n * CHUNK, (n + 1) * CHUNK)
            gated = u_ref[rows, cols].astype(F32) * (mixed[:, n * CHUNK:(n + 1) * CHUNK] + bs_ref[g])
            t_scr[rows, cols] = gated.astype(BF16)
    _skewed(
        _row_chunks(u_ref.shape[0]),
        lambda rows: _dot(t_scr[rows, :], wo_ref[...]),
        lambda rows, mix: _residual_and_ffn_norm(rows, i // tiles_per_mod, x_ref, mix, gate_ref, gn_ref,
                                                 shf_ref, scf_ref, x1_ref, h_ref))


def _odd_out(u, v, w_s, b_s, w_out, x2d, mods, norm_ffn_g, layer, seq, tm):
    m = x2d.shape[0]
    kern = functools.partial(_odd_out_kernel, tiles_per_mod=seq // tm)
    return _pallas(
        kern,
        name="odd_out",
        grid=(m // tm,),
        in_specs=[
            pl.BlockSpec((tm, D_MODEL), lambda i: (i, 0)),
            pl.BlockSpec((tm, D_MODEL), lambda i: (i, 0)),
            _resident_spec(w_s.shape),
            _resident_spec(b_s.shape),
            _resident_spec(w_out.shape),
            pl.BlockSpec((tm, D_MODEL), lambda i: (i, 0)),
            _mod_spec(layer, 2, 1),
            pl.BlockSpec((None, 1, D_MODEL), lambda i: (layer, 0, 0)),
            _mod_spec(layer, 3, 1),
            _mod_spec(layer, 4, 1),
        ],
        args=(u, v, w_s, b_s, w_out, x2d, mods, norm_ffn_g, mods, mods),
        out_specs=[
            pl.BlockSpec((tm, D_MODEL), lambda i: (i, 0)),
            pl.BlockSpec((tm, D_MODEL), lambda i: (i, 0)),
        ],
        out_shape=[
            jax.ShapeDtypeStruct((m, D_MODEL), F32),
            jax.ShapeDtypeStruct((m, D_MODEL), BF16),
        ],
        scratch_shapes=[pltpu.VMEM((tm, D_MODEL), BF16)],
        temp_bytes=2 * 2 * _nbytes((ROW_CHUNK, D_MODEL), F32),
    )


def kernel(x, c, ctx, c_ctx, ada_w, ada_b, norm_mix_g, norm_ffn_g, ffn_w_gate, ffn_w_up, ffn_w_down,
           even_w_in, even_w_out, rpb, conv_w, odd_w_in, odd_w_out, sg_ln_g, sg_ln_b, sg_w, sg_b,
           final_norm_g):
    batch, seq, d = x.shape
    ctx_len = ctx.shape[1]
    depth = ada_w.shape[0]
    assert d == D_MODEL and depth == 2 and batch < MOD_ROWS and seq % (GRID_W * WIN_ROWS) == 0

    bf = lambda w: w.astype(BF16)
    x2d = x.reshape(batch * seq, d)
    ctx2d = ctx.reshape(batch * ctx_len, d)
    g_mix = norm_mix_g.reshape(depth, 1, d)
    g_ffn = norm_ffn_g.reshape(depth, 1, d)

    cond = jnp.concatenate([c, c_ctx[None], jnp.zeros((MOD_ROWS - batch - 1, d), F32)], axis=0)
    mods = _ada_mods(cond, ada_w, ada_b)

    ffn_casts = lambda layer: [(ffn_w_gate, layer, FFN_TILE), (ffn_w_up, layer, FFN_TILE),
                               (ffn_w_down, layer, None)]
    w_in0 = bf(even_w_in[0])
    proj, w_out0, wg0, wu0, wd0 = _norm_matmul(
        x2d, g_mix, mods, 0, w_in0, col0=0, n_cols=w_in0.shape[1], rows_per_mod=seq, mod_row0=0,
        tm=1024, tn=1536, name="even_in", cast_weights=[(even_w_out, 0, None)] + ffn_casts(0))
    ctx_kv, = _norm_matmul(ctx2d, g_mix, mods, 0, w_in0, col0=NA_WIDTH, n_cols=2 * NA_WIDTH,
                           rows_per_mod=batch * ctx_len, mod_row0=batch, tm=1024, tn=1024, name="ctx_kv")
    bias, cap = _bias_tables(rpb[0])
    attn, w_in1, w_out1, wg1, wu1, wd1 = _attention(
        proj, ctx_kv, bias, cap, batch, seq, ctx_len,
        cast_weights=[(odd_w_in, 0, None), (odd_w_out, 0, None)] + ffn_casts(1))
    x1, h = _even_out(attn, proj, conv_w[0], w_out0, x2d, mods, g_ffn, 0, seq, tm=512)
    x2, h = _ffn(h, wg0, wu0, wd0, x1, mods, 0, g_mix, 1, seq, tm=1024)

    u, v = _odd_in(h, w_in1, sg_ln_g[0][None], sg_ln_b[0][None], tm=512)
    b_s = jnp.broadcast_to(sg_b[0][:, :, None], (SG_GROUPS, CHUNK, CHUNK))
    x1, h = _odd_out(u, v, bf(sg_w[0]), b_s, w_out1, x2, mods, g_ffn, 1, seq, tm=512)
    out = _ffn(h, wg1, wu1, wd1, x1, mods, 1, final_norm_g[None], None, seq, tm=1024)
    return out.reshape(batch, seq, d)
```

```python
import functools

import jax
import jax.numpy as jnp
from jax import lax
from jax.experimental import pallas as pl
from jax.experimental.pallas import tpu as pltpu

D_MODEL = 2048
GRID_W = 64
NA_HEADS = 8
HEAD_DIM = 128
NA_WIDTH = NA_HEADS * HEAD_DIM
CONV_CH = D_MODEL - NA_WIDTH
WIN_ROWS = 8
WIN_COLS = 16
CHUNK = 128
SG_GROUPS = 16
EPS = 1e-6
NEG_INF = -1e30
UNMASKED_CAP = 3e38
MOD_ROWS = 16
ATTN_HEADS_PER_STEP = 2
NORM_SLICES = 8
FFN_TILE = 512
FFN_ROW_CHUNKS = 4
FFN_EPILOGUE_SLICES = 8
ROW_CHUNK = 256

F32 = jnp.float32
BF16 = jnp.bfloat16
BF16_SUBLANES = 16

MIB = 1024 * 1024
V7X_VMEM_BYTES = 64 * MIB
VMEM_CAP_BYTES = V7X_VMEM_BYTES - 5 * MIB
VMEM_SLACK_BYTES = 2 * MIB


def _nbytes(shape, dtype):
    n = jnp.dtype(dtype).itemsize
    for d in shape:
        n *= 1 if d is None else d
    return n


def _pallas(kernel_fn, *, name, grid, in_specs, args, out_specs, out_shape, scratch_shapes=(), temp_bytes):
    outs = list(zip(out_specs, out_shape)) if isinstance(out_specs, (list, tuple)) else [(out_specs, out_shape)]
    total = temp_bytes + VMEM_SLACK_BYTES
    for spec, arg in zip(in_specs, args):
        if spec.block_shape is not None:
            n_buffers = 2 if spec.pipeline_mode is None else spec.pipeline_mode.buffer_count
            total += n_buffers * _nbytes(spec.block_shape, arg.dtype)
    for spec, shape in outs:
        total += 2 * _nbytes(spec.block_shape, shape.dtype)
    for s in scratch_shapes:
        total += _nbytes(s.shape, s.dtype)
    assert total <= VMEM_CAP_BYTES, (name, total)
    return pl.pallas_call(
        kernel_fn, grid=grid, in_specs=in_specs, out_specs=out_specs, out_shape=out_shape,
        scratch_shapes=list(scratch_shapes),
        compiler_params=pltpu.CompilerParams(dimension_semantics=("arbitrary",) * len(grid),
                                             vmem_limit_bytes=total),
        name=name,
    )(*args)


def _resident_spec(shape):
    zeros = (0,) * len(shape)
    return pl.BlockSpec(shape, lambda *_: zeros, pipeline_mode=pl.Buffered(1))


def _dot(a, b):
    return jnp.dot(a, b, preferred_element_type=F32)


def _norm_mod(x, g, shift, scale):
    y = x * lax.rsqrt(jnp.mean(x * x, axis=-1, keepdims=True) + EPS)
    return (y * g) * (1.0 + scale) + shift


def _linear_step(ids, grid):
    step = ids[0]
    for axis in range(1, len(grid)):
        step = step * grid[axis] + ids[axis]
    return step


class _Cast:
    def __init__(self, stacked_w, layer, grid, col_tile=None):
        _, rows, cols = stacked_w.shape
        n_steps = 1
        for g in grid:
            n_steps *= g
        self.w = stacked_w
        self.grid = grid
        self.col_tile = col_tile
        self.n_blocks = nb = max(n for n in range(1, n_steps + 1)
                                 if rows % n == 0 and (rows // n) % BF16_SUBLANES == 0)
        rb = rows // nb
        block = lambda *ids: jnp.minimum(_linear_step(ids, grid), nb - 1)
        self.in_spec = pl.BlockSpec((None, rb, cols), lambda *ids: (layer, block(*ids), 0))
        if col_tile is None:
            self.out_shape = jax.ShapeDtypeStruct((rows, cols), BF16)
            self.out_spec = pl.BlockSpec((rb, cols), lambda *ids: (block(*ids), 0))
        else:
            nt = cols // col_tile
            self.out_shape = jax.ShapeDtypeStruct((nt, rows, col_tile), BF16)
            self.out_spec = pl.BlockSpec((nt, rb, col_tile), lambda *ids: (0, block(*ids), 0))

    def run(self, w_ref, o_ref):
        step = _linear_step([pl.program_id(a) for a in range(len(self.grid))], self.grid)

        @pl.when(step < self.n_blocks)
        def _():
            if self.col_tile is None:
                o_ref[...] = w_ref[...].astype(BF16)
            else:
                for t in range(o_ref.shape[0]):
                    o_ref[t] = w_ref[:, t * self.col_tile:(t + 1) * self.col_tile].astype(BF16)


def _with_casts(body, n_in, n_out, casts):
    n_c = len(casts)

    def kern(*refs):
        ins, rest = refs[:n_in], refs[n_in:]
        cast_ins, rest = rest[:n_c], rest[n_c:]
        outs, rest = rest[:n_out], rest[n_out:]
        cast_outs, scratch = rest[:n_c], rest[n_c:]
        body(*ins, *outs, *scratch)
        for cast, w_ref, o_ref in zip(casts, cast_ins, cast_outs):
            cast.run(w_ref, o_ref)

    return kern


def _ada_kernel(cond_ref, w_ref, b_ref, o_ref):
    a = jax.nn.silu(cond_ref[...]).astype(BF16)
    o_ref[...] = _dot(a, w_ref[...].astype(BF16)) + b_ref[...]


def _ada_mods(cond, ada_w, ada_b):
    depth, _, n = ada_w.shape
    tn = 1024
    return _pallas(
        _ada_kernel,
        name="ada_mods",
        grid=(depth, n // tn),
        in_specs=[
            pl.BlockSpec((MOD_ROWS, D_MODEL), lambda l, j: (0, 0)),
            pl.BlockSpec((None, D_MODEL, tn), lambda l, j: (l, 0, j)),
            pl.BlockSpec((None, 1, tn), lambda l, j: (l, 0, j)),
        ],
        args=(cond, ada_w, ada_b.reshape(depth, 1, n)),
        out_specs=pl.BlockSpec((None, MOD_ROWS, tn), lambda l, j: (l, 0, j)),
        out_shape=jax.ShapeDtypeStruct((depth, MOD_ROWS, n), F32),
        temp_bytes=_nbytes((D_MODEL, tn), BF16),
    )


def _mod_spec(layer, chunk, n_grid_axes):
    if n_grid_axes == 1:
        return pl.BlockSpec((None, MOD_ROWS, D_MODEL), lambda i: (layer, 0, chunk))
    return pl.BlockSpec((None, MOD_ROWS, D_MODEL), lambda i, j: (layer, 0, chunk))


def _norm_matmul_kernel(x_ref, g_ref, sh_ref, sc_ref, w_ref, o_ref, h_scr, *,
                        n_row_tiles, n_slices, tiles_per_mod, mod_row0):
    i = pl.program_id(0)
    j = pl.program_id(1)
    rs = x_ref.shape[0]

    def norm_slice():
        r0 = pl.multiple_of(jnp.minimum(j, n_slices - 1) * rs, rs)
        row = mod_row0 + i // tiles_per_mod
        h = _norm_mod(x_ref[...], g_ref[...], sh_ref[pl.ds(row, 1), :], sc_ref[pl.ds(row, 1), :])
        h_scr[i % 2, pl.ds(r0, rs), :] = h.astype(BF16)

    def matmul():
        o_ref[...] = _dot(h_scr[(i - 1) % 2], w_ref[...]).astype(o_ref.dtype)

    @pl.when(i == 0)
    def _():
        norm_slice()

    @pl.when((i > 0) & (i < n_row_tiles))
    def _():
        matmul()
        norm_slice()

    @pl.when(i == n_row_tiles)
    def _():
        matmul()


def _norm_matmul(x2d, g, mods, layer, w, *, col0, n_cols, rows_per_mod, mod_row0, tm, tn, name,
                 cast_weights=()):
    m = x2d.shape[0]
    n_i, n_j = m // tm, n_cols // tn
    n_s = min(n_j, NORM_SLICES)
    grid = (n_i + 1, n_j)
    body = functools.partial(_norm_matmul_kernel, n_row_tiles=n_i, n_slices=n_s,
                             tiles_per_mod=max(rows_per_mod // tm, 1), mod_row0=mod_row0)
    casts = [_Cast(cw, layer_, grid, col_tile) for cw, layer_, col_tile in cast_weights]
    cb0 = col0 // tn
    x_block = lambda i, j: jnp.minimum(i, n_i - 1) * n_s + jnp.minimum(j, n_s - 1)
    w_block = lambda i, j: cb0 + jnp.where(i == 0, 0, j)
    return _pallas(
        _with_casts(body, 5, 1, casts),
        name=name,
        grid=grid,
        in_specs=[
            pl.BlockSpec((tm // n_s, D_MODEL), lambda i, j: (x_block(i, j), 0)),
            pl.BlockSpec((None, 1, D_MODEL), lambda i, j: (layer, 0, 0)),
            _mod_spec(layer, 0, 2),
            _mod_spec(layer, 1, 2),
            pl.BlockSpec((D_MODEL, tn), lambda i, j: (0, w_block(i, j))),
        ] + [c.in_spec for c in casts],
        args=(x2d, g, mods, mods, w, *[c.w for c in casts]),
        out_specs=[pl.BlockSpec((tm, tn), lambda i, j: (jnp.maximum(i - 1, 0), jnp.where(i == 0, 0, j)))]
        + [c.out_spec for c in casts],
        out_shape=[jax.ShapeDtypeStruct((m, n_cols), BF16)] + [c.out_shape for c in casts],
        scratch_shapes=[pltpu.VMEM((2, tm, D_MODEL), BF16)],
        temp_bytes=_nbytes((tm, tn), F32) + 3 * _nbytes((tm // n_s, D_MODEL), F32),
    )


def _bias_kernel(rpb_ref, bias_ref, cap_ref):
    h = pl.program_id(0)
    n_dc = 2 * WIN_COLS - 1
    n_dr = 2 * WIN_ROWS - 1
    q = lax.broadcasted_iota(jnp.int32, (GRID_W, 2 * GRID_W), 0)
    lane = lax.broadcasted_iota(jnp.int32, (GRID_W, 2 * GRID_W), 1)
    second = lane >= GRID_W
    k = jnp.where(second, lane - GRID_W, lane)
    dc = jnp.clip(k - q, -(WIN_COLS - 1), WIN_COLS - 1) + (WIN_COLS - 1)
    pairs = []
    for m in range(n_dr - 1):
        acc = jnp.zeros((GRID_W, 2 * GRID_W), F32)
        for d in range(n_dc):
            val = jnp.where(second, rpb_ref[h, (m + 1) * n_dc + d], rpb_ref[h, m * n_dc + d])
            acc = jnp.where(dc == d, val, acc)
        pairs.append(acc)
    for o in range(WIN_ROWS):
        for jj in range(WIN_ROWS // 2):
            bias_ref[o, :, jj * 2 * GRID_W:(jj + 1) * 2 * GRID_W] = pairs[o + 2 * jj]
    col_start = jnp.clip(q - WIN_COLS // 2, 0, GRID_W - WIN_COLS)
    cap = jnp.where((k >= col_start) & (k < col_start + WIN_COLS), UNMASKED_CAP, NEG_INF).astype(F32)
    for jj in range(WIN_ROWS // 2):
        cap_ref[:, jj * 2 * GRID_W:(jj + 1) * 2 * GRID_W] = cap


def _bias_tables(rpb):
    h = rpb.shape[0]
    kw = WIN_ROWS * GRID_W
    return _pallas(
        _bias_kernel,
        name="bias_tables",
        grid=(h,),
        in_specs=[pl.BlockSpec(memory_space=pltpu.SMEM)],
        args=(rpb.reshape(h, -1),),
        out_specs=[
            pl.BlockSpec((None, WIN_ROWS, GRID_W, kw), lambda i: (i, 0, 0, 0)),
            pl.BlockSpec((GRID_W, kw), lambda i: (0, 0)),
        ],
        out_shape=[
            jax.ShapeDtypeStruct((h, WIN_ROWS, GRID_W, kw), F32),
            jax.ShapeDtypeStruct((GRID_W, kw), F32),
        ],
        temp_bytes=2 * WIN_ROWS * _nbytes((GRID_W, 2 * GRID_W), F32),
    )


def _attn_kernel(q_ref, k_ref, v_ref, kc_ref, vc_ref, bias_ref, cap_ref, o_ref, *scratch, rows):
    scale = HEAD_DIM ** -0.5
    kw = WIN_ROWS * GRID_W
    ctx_len = kc_ref.shape[0]
    nt = (((1,), (1,)), ((), ()))
    per_head = len(scratch) // ATTN_HEADS_PER_STEP

    def parts(head):
        return scratch[head * per_head:(head + 1) * per_head], slice(head * HEAD_DIM, (head + 1) * HEAD_DIM)

    def setup(head):
        (s_scr, _, vx_scr, vcx_scr, _), lanes = parts(head)
        vx_scr[:, :HEAD_DIM] = v_ref[:, lanes]
        vx_scr[:, HEAD_DIM:] = jnp.ones((v_ref.shape[0], HEAD_DIM), BF16)
        vcx_scr[:, :HEAD_DIM] = vc_ref[:, lanes]
        vcx_scr[:, HEAD_DIM:] = jnp.ones((ctx_len, HEAD_DIM), BF16)
        s_scr[:, kw:] = lax.dot_general(q_ref[:, lanes], kc_ref[:, lanes], nt,
                                        preferred_element_type=F32) * scale

    def scores(head, r):
        (s_scr, _, _, _, _), lanes = parts(head)
        row_start = jnp.clip(r - WIN_ROWS // 2, 0, rows - WIN_ROWS)
        o = row_start - r + (WIN_ROWS - 1)
        q0 = pl.multiple_of(r * GRID_W, GRID_W)
        k0 = pl.multiple_of(row_start * GRID_W, GRID_W)
        s = lax.dot_general(q_ref[pl.ds(q0, GRID_W), lanes], k_ref[pl.ds(k0, kw), lanes], nt,
                            preferred_element_type=F32)
        s_scr[pl.ds(q0, GRID_W), :kw] = jnp.minimum(s * scale + bias_ref[head, o], cap_ref[...])

    def expo(head, r):
        (s_scr, e_scr, _, _, _), _ = parts(head)
        q0 = pl.multiple_of(r * GRID_W, GRID_W)
        s = s_scr[pl.ds(q0, GRID_W), :]
        e_scr[pl.ds(q0, GRID_W), :] = jnp.exp(s - jnp.max(s, axis=-1, keepdims=True)).astype(BF16)

    def ctx_values(head):
        (_, e_scr, _, vcx_scr, acc_scr), _ = parts(head)
        acc_scr[...] = _dot(e_scr[:, kw:], vcx_scr[...])

    def values(head, r):
        (_, e_scr, vx_scr, _, acc_scr), lanes = parts(head)
        row_start = jnp.clip(r - WIN_ROWS // 2, 0, rows - WIN_ROWS)
        q0 = pl.multiple_of(r * GRID_W, GRID_W)
        k0 = pl.multiple_of(row_start * GRID_W, GRID_W)
        t = acc_scr[pl.ds(q0, GRID_W), :] + _dot(e_scr[pl.ds(q0, GRID_W), :kw], vx_scr[pl.ds(k0, kw), :])
        o_ref[pl.ds(q0, GRID_W), lanes] = (t[:, :HEAD_DIM] / t[:, HEAD_DIM:]).astype(o_ref.dtype)

    def loop(*row_fns):
        def body(r, carry):
            for fn in row_fns:
                fn(r)
            return carry
        lax.fori_loop(0, rows, body, 0, unroll=16)

    setup(0)
    setup(1)
    loop(functools.partial(scores, 0))
    loop(functools.partial(expo, 0), functools.partial(scores, 1))
    ctx_values(0)
    loop(functools.partial(expo, 1), functools.partial(values, 0))
    ctx_values(1)
    loop(functools.partial(values, 1))


def _attention(qkv, ctx_kv, bias, cap, batch, seq, ctx_len, cast_weights=()):
    rows = seq // GRID_W
    kw = WIN_ROWS * GRID_W
    hps = ATTN_HEADS_PER_STEP
    n_hg = NA_HEADS // hps
    grid = (n_hg, batch)
    casts = [_Cast(cw, layer_, grid, col_tile) for cw, layer_, col_tile in cast_weights]
    return _pallas(
        _with_casts(functools.partial(_attn_kernel, rows=rows), 7, 1, casts),
        name="nbr_attention",
        grid=grid,
        in_specs=[
            pl.BlockSpec((seq, hps * HEAD_DIM), lambda h, b: (b, h)),
            pl.BlockSpec((seq, hps * HEAD_DIM), lambda h, b: (b, n_hg + h)),
            pl.BlockSpec((seq, hps * HEAD_DIM), lambda h, b: (b, 2 * n_hg + h)),
            pl.BlockSpec((ctx_len, hps * HEAD_DIM), lambda h, b: (b, h)),
            pl.BlockSpec((ctx_len, hps * HEAD_DIM), lambda h, b: (b, n_hg + h)),
            pl.BlockSpec((hps, WIN_ROWS, GRID_W, kw), lambda h, b: (h, 0, 0, 0)),
            pl.BlockSpec((GRID_W, kw), lambda h, b: (0, 0)),
        ] + [c.in_spec for c in casts],
        args=(qkv, qkv, qkv, ctx_kv, ctx_kv, bias, cap, *[c.w for c in casts]),
        out_specs=[pl.BlockSpec((seq, hps * HEAD_DIM), lambda h, b: (b, h))] + [c.out_spec for c in casts],
        out_shape=[jax.ShapeDtypeStruct((batch * seq, NA_WIDTH), BF16)] + [c.out_shape for c in casts],
        scratch_shapes=[
            pltpu.VMEM((seq, kw + ctx_len), F32),
            pltpu.VMEM((seq, kw + ctx_len), BF16),
            pltpu.VMEM((seq, 2 * HEAD_DIM), BF16),
            pltpu.VMEM((ctx_len, 2 * HEAD_DIM), BF16),
            pltpu.VMEM((seq, 2 * HEAD_DIM), F32),
        ] * hps,
        temp_bytes=_nbytes((seq, ctx_len), F32),
    )


def _residual_and_ffn_norm(rows, mod_row, x_ref, mix, gate_ref, gn_ref, shf_ref, scf_ref, x1_ref, h_ref):
    x1 = x_ref[rows, :] + gate_ref[pl.ds(mod_row, 1), :] * mix
    x1_ref[rows, :] = x1
    h = _norm_mod(x1, gn_ref[...], shf_ref[pl.ds(mod_row, 1), :], scf_ref[pl.ds(mod_row, 1), :])
    h_ref[rows, :] = h.astype(BF16)


def _row_chunks(tm):
    return [slice(r, r + ROW_CHUNK) for r in range(0, tm, ROW_CHUNK)]


def _skewed(chunks, matmul_fn, epilogue_fn):
    pending = None
    for rows in chunks:
        acc = matmul_fn(rows)
        if pending is not None:
            epilogue_fn(*pending)
        pending = (rows, acc)
    epilogue_fn(*pending)


def _even_out_kernel(attn_ref, b_ref, c_ref, u_ref, cp_ref, up_ref, cn_ref, un_ref, cw_ref, wo_ref,
                     x_ref, gate_ref, gn_ref, shf_ref, scf_ref, x1_ref, h_ref, conv_scr, *, tiles_per_mod, halo):
    i = pl.program_id(0)
    tm = c_ref.shape[0]
    t = i % tiles_per_mod
    conv_scr[:, :NA_WIDTH] = attn_ref[...]
    z = c_ref[...].astype(F32) * u_ref[...].astype(F32)
    z_before = cp_ref[halo - 1:halo, :].astype(F32) * up_ref[halo - 1:halo, :].astype(F32)
    z_after = cn_ref[0:1, :].astype(F32) * un_ref[0:1, :].astype(F32)
    z_before = jnp.where(t == 0, 0.0, z_before)
    z_after = jnp.where(t == tiles_per_mod - 1, 0.0, z_after)
    ridx = lax.broadcasted_iota(jnp.int32, (tm, 1), 0)
    z_prev = jnp.where(ridx == 0, z_before, pltpu.roll(z, 1, 0))
    z_next = jnp.where(ridx == tm - 1, z_after, pltpu.roll(z, tm - 1, 0))
    y = z_prev * cw_ref[0:1, :] + z * cw_ref[1:2, :] + z_next * cw_ref[2:3, :]
    conv_scr[:, NA_WIDTH:] = (b_ref[...].astype(F32) * y).astype(BF16)
    _skewed(
        _row_chunks(tm),
        lambda rows: _dot(conv_scr[rows, :], wo_ref[...]),
        lambda rows, mix: _residual_and_ffn_norm(rows, i // tiles_per_mod, x_ref, mix, gate_ref, gn_ref,
                                                 shf_ref, scf_ref, x1_ref, h_ref))


def _even_out(attn, proj, conv_w, w_out, x2d, mods, norm_ffn_g, layer, seq, tm):
    m = x2d.shape[0]
    halo = 16
    hb = tm // halo
    n_halo_blocks = m // halo
    cb = NA_WIDTH * 3 // CONV_CH
    kern = functools.partial(_even_out_kernel, tiles_per_mod=seq // tm, halo=halo)
    main = lambda c: pl.BlockSpec((tm, CONV_CH), lambda i: (i, c))
    prev = lambda c: pl.BlockSpec((halo, CONV_CH), lambda i: (jnp.maximum(i * hb - 1, 0), c))
    nxt = lambda c: pl.BlockSpec((halo, CONV_CH), lambda i: (jnp.minimum((i + 1) * hb, n_halo_blocks - 1), c))
    return _pallas(
        kern,
        name="even_out",
        grid=(m // tm,),
        in_specs=[
            pl.BlockSpec((tm, NA_WIDTH), lambda i: (i, 0)),
            main(cb), main(cb + 1), main(cb + 2),
            prev(cb + 1), prev(cb + 2), nxt(cb + 1), nxt(cb + 2),
            _resident_spec(conv_w.shape),
            _resident_spec(w_out.shape),
            pl.BlockSpec((tm, D_MODEL), lambda i: (i, 0)),
            _mod_spec(layer, 2, 1),
            pl.BlockSpec((None, 1, D_MODEL), lambda i: (layer, 0, 0)),
            _mod_spec(layer, 3, 1),
            _mod_spec(layer, 4, 1),
        ],
        args=(attn, proj, proj, proj, proj, proj, proj, proj, conv_w, w_out, x2d, mods, norm_ffn_g, mods, mods),
        out_specs=[
            pl.BlockSpec((tm, D_MODEL), lambda i: (i, 0)),
            pl.BlockSpec((tm, D_MODEL), lambda i: (i, 0)),
        ],
        out_shape=[
            jax.ShapeDtypeStruct((m, D_MODEL), F32),
            jax.ShapeDtypeStruct((m, D_MODEL), BF16),
        ],
        scratch_shapes=[pltpu.VMEM((tm, NA_WIDTH + CONV_CH), BF16)],
        temp_bytes=3 * _nbytes((tm, CONV_CH), F32) + 2 * _nbytes((tm, D_MODEL), F32),
    )


def _ffn_kernel(*refs, n_row_tiles, tiles_per_mod, modulated):
    if modulated:
        (h_ref, wg_ref, wu_ref, wd_ref, x_ref, gate_ref, gn_ref, sh_ref, sc_ref,
         x2_ref, y_ref, acc_ref) = refs
    else:
        h_ref, wg_ref, wu_ref, wd_ref, x_ref, gate_ref, gn_ref, y_ref, acc_ref = refs
    i = pl.program_id(0)
    f = pl.program_id(1)
    slot = i % 2
    tm = h_ref.shape[0]
    rs = tm // FFN_EPILOGUE_SLICES

    @pl.when((f == 0) & (i < n_row_tiles))
    def _():
        acc_ref[slot] = jnp.zeros((tm, D_MODEL), F32)

    def matmuls():
        rc = tm // FFN_ROW_CHUNKS
        ts = []
        for c in range(FFN_ROW_CHUNKS):
            h = h_ref[c * rc:(c + 1) * rc, :]
            ts.append((jax.nn.silu(_dot(h, wg_ref[...])) * _dot(h, wu_ref[...])).astype(BF16))
        for c in range(FFN_ROW_CHUNKS):
            acc_ref[slot, c * rc:(c + 1) * rc, :] += _dot(ts[c], wd_ref[...])

    def epilogue_slice():
        r0 = pl.multiple_of(f * rs, rs)
        row = (i - 1) // tiles_per_mod
        x2 = x_ref[...] + gate_ref[pl.ds(row, 1), :] * acc_ref[1 - slot, pl.ds(r0, rs), :]
        if modulated:
            x2_ref[...] = x2
            y = _norm_mod(x2, gn_ref[...], sh_ref[pl.ds(row, 1), :], sc_ref[pl.ds(row, 1), :])
        else:
            y = x2 * lax.rsqrt(jnp.mean(x2 * x2, axis=-1, keepdims=True) + EPS) * gn_ref[...]
        y_ref[...] = y.astype(y_ref.dtype)

    has_slice = (i > 0) & (f < FFN_EPILOGUE_SLICES)
    has_matmuls = i < n_row_tiles

    @pl.when(has_matmuls & jnp.logical_not(has_slice))
    def _():
        matmuls()

    @pl.when(has_matmuls & has_slice)
    def _():
        epilogue_slice()
        matmuls()

    @pl.when(jnp.logical_not(has_matmuls) & has_slice)
    def _():
        epilogue_slice()


def _ffn(h, w_gate, w_up, w_down, x2d, mods, layer, next_g, next_layer, seq, tm):
    m = x2d.shape[0]
    n_f, _, tf = w_gate.shape
    n_i = m // tm
    n_s = FFN_EPILOGUE_SLICES
    assert n_f >= n_s and (tm // n_s) % BF16_SUBLANES == 0
    modulated = next_layer is not None
    kern = functools.partial(_ffn_kernel, n_row_tiles=n_i, tiles_per_mod=seq // tm, modulated=modulated)
    w_tile = lambda i, f: jnp.where(i == n_i, n_f - 1, f)
    e_block = lambda i, f: jnp.where(i == 0, 0, (i - 1) * n_s + jnp.minimum(f, n_s - 1))
    slice_spec = pl.BlockSpec((tm // n_s, D_MODEL), lambda i, f: (e_block(i, f), 0))
    in_specs = [
        pl.BlockSpec((tm, D_MODEL), lambda i, f: (jnp.minimum(i, n_i - 1), 0)),
        pl.BlockSpec((None, D_MODEL, tf), lambda i, f: (w_tile(i, f), 0, 0)),
        pl.BlockSpec((None, D_MODEL, tf), lambda i, f: (w_tile(i, f), 0, 0)),
        pl.BlockSpec((tf, D_MODEL), lambda i, f: (w_tile(i, f), 0)),
        slice_spec,
        _mod_spec(layer, 5, 2),
    ]
    args = [h, w_gate, w_up, w_down, x2d, mods]
    if modulated:
        in_specs += [pl.BlockSpec((None, 1, D_MODEL), lambda i, f: (next_layer, 0, 0)),
                     _mod_spec(next_layer, 0, 2), _mod_spec(next_layer, 1, 2)]
        args += [next_g, mods, mods]
        out_specs = [slice_spec, slice_spec]
        out_shape = [jax.ShapeDtypeStruct((m, D_MODEL), F32), jax.ShapeDtypeStruct((m, D_MODEL), BF16)]
    else:
        in_specs += [pl.BlockSpec((1, D_MODEL), lambda i, f: (0, 0))]
        args += [next_g]
        out_specs = slice_spec
        out_shape = jax.ShapeDtypeStruct((m, D_MODEL), F32)
    rc = tm // FFN_ROW_CHUNKS
    return _pallas(
        kern,
        name="ffn_final" if not modulated else "ffn",
        grid=(n_i + 1, n_f),
        in_specs=in_specs,
        args=args,
        out_specs=out_specs,
        out_shape=out_shape,
        scratch_shapes=[pltpu.VMEM((2, tm, D_MODEL), F32)],
        temp_bytes=FFN_ROW_CHUNKS * (2 * _nbytes((rc, tf), F32) + _nbytes((rc, tf), BF16))
        + _nbytes((rc, D_MODEL), F32),
    )


def _odd_in_kernel(h_ref, w_ref, lg_ref, lb_ref, u_ref, v_ref):
    def matmuls(rows):
        h = h_ref[rows, :]
        return _dot(h, w_ref[:, :D_MODEL]), _dot(h, w_ref[:, D_MODEL:])

    def epilogue(rows, acc):
        u_ref[rows, :] = jax.nn.gelu(acc[0], approximate=True).astype(BF16)
        z = jax.nn.gelu(acc[1], approximate=True)
        zc = z - jnp.mean(z, axis=-1, keepdims=True)
        var = jnp.mean(zc * zc, axis=-1, keepdims=True)
        v_ref[rows, :] = (zc * lax.rsqrt(var + EPS) * lg_ref[...] + lb_ref[...]).astype(BF16)

    _skewed(_row_chunks(h_ref.shape[0]), matmuls, epilogue)


def _odd_in(h, w_in, ln_g, ln_b, tm):
    m = h.shape[0]
    row_spec = pl.BlockSpec((tm, D_MODEL), lambda i: (i, 0))
    return _pallas(
        _odd_in_kernel,
        name="odd_in",
        grid=(m // tm,),
        in_specs=[
            row_spec,
            _resident_spec(w_in.shape),
            pl.BlockSpec((1, D_MODEL), lambda i: (0, 0)),
            pl.BlockSpec((1, D_MODEL), lambda i: (0, 0)),
        ],
        args=(h, w_in, ln_g, ln_b),
        out_specs=[row_spec, row_spec],
        out_shape=[jax.ShapeDtypeStruct((m, D_MODEL), BF16)] * 2,
        temp_bytes=2 * 4 * _nbytes((ROW_CHUNK, D_MODEL), F32),
    )


def _odd_out_kernel(u_ref, v_ref, ws_ref, bs_ref, wo_ref, x_ref, gate_ref, gn_ref, shf_ref, scf_ref,
                    x1_ref, h_ref, t_scr, *, tiles_per_mod):
    i = pl.program_id(0)
    n_chunks = u_ref.shape[0] // CHUNK
    for g in range(SG_GROUPS):
        cols = slice(g * CHUNK, (g + 1) * CHUNK)
        vcat = jnp.concatenate([v_ref[n * CHUNK:(n + 1) * CHUNK, cols] for n in range(n_chunks)], axis=1)
        mixed = _dot(ws_ref[g], vcat)
        for n in range(n_chunks):
            rows = slice(n * CHUNK, (n + 1) * CHUNK)
            gated = u_ref[rows, cols].astype(F32) * (mixed[:, n * CHUNK:(n + 1) * CHUNK] + bs_ref[g])
            t_scr[rows, cols] = gated.astype(BF16)
    _skewed(
        _row_chunks(u_ref.shape[0]),
        lambda rows: _dot(t_scr[rows, :], wo_ref[...]),
        lambda rows, mix: _residual_and_ffn_norm(rows, i // tiles_per_mod, x_ref, mix, gate_ref, gn_ref,
                                                 shf_ref, scf_ref, x1_ref, h_ref))


def _odd_out(u, v, w_s, b_s, w_out, x2d, mods, norm_ffn_g, layer, seq, tm):
    m = x2d.shape[0]
    kern = functools.partial(_odd_out_kernel, tiles_per_mod=seq // tm)
    return _pallas(
        kern,
        name="odd_out",
        grid=(m // tm,),
        in_specs=[
            pl.BlockSpec((tm, D_MODEL), lambda i: (i, 0)),
            pl.BlockSpec((tm, D_MODEL), lambda i: (i, 0)),
            _resident_spec(w_s.shape),
            _resident_spec(b_s.shape),
            _resident_spec(w_out.shape),
            pl.BlockSpec((tm, D_MODEL), lambda i: (i, 0)),
            _mod_spec(layer, 2, 1),
            pl.BlockSpec((None, 1, D_MODEL), lambda i: (layer, 0, 0)),
            _mod_spec(layer, 3, 1),
            _mod_spec(layer, 4, 1),
        ],
        args=(u, v, w_s, b_s, w_out, x2d, mods, norm_ffn_g, mods, mods),
        out_specs=[
            pl.BlockSpec((tm, D_MODEL), lambda i: (i, 0)),
            pl.BlockSpec((tm, D_MODEL), lambda i: (i, 0)),
        ],
        out_shape=[
            jax.ShapeDtypeStruct((m, D_MODEL), F32),
            jax.ShapeDtypeStruct((m, D_MODEL), BF16),
        ],
        scratch_shapes=[pltpu.VMEM((tm, D_MODEL), BF16)],
        temp_bytes=2 * 2 * _nbytes((ROW_CHUNK, D_MODEL), F32),
    )


def kernel(x, c, ctx, c_ctx, ada_w, ada_b, norm_mix_g, norm_ffn_g, ffn_w_gate, ffn_w_up, ffn_w_down,
           even_w_in, even_w_out, rpb, conv_w, odd_w_in, odd_w_out, sg_ln_g, sg_ln_b, sg_w, sg_b,
           final_norm_g):
    batch, seq, d = x.shape
    ctx_len = ctx.shape[1]
    depth = ada_w.shape[0]
    assert d == D_MODEL and depth == 2 and batch < MOD_ROWS and seq % (GRID_W * WIN_ROWS) == 0

    bf = lambda w: w.astype(BF16)
    x2d = x.reshape(batch * seq, d)
    ctx2d = ctx.reshape(batch * ctx_len, d)
    g_mix = norm_mix_g.reshape(depth, 1, d)
    g_ffn = norm_ffn_g.reshape(depth, 1, d)

    cond = jnp.concatenate([c, c_ctx[None], jnp.zeros((MOD_ROWS - batch - 1, d), F32)], axis=0)
    mods = _ada_mods(cond, ada_w, ada_b)

    ffn_casts = lambda layer: [(ffn_w_gate, layer, FFN_TILE), (ffn_w_up, layer, FFN_TILE),
                               (ffn_w_down, layer, None)]
    w_in0 = bf(even_w_in[0])
    proj, w_out0, wg0, wu0, wd0 = _norm_matmul(
        x2d, g_mix, mods, 0, w_in0, col0=0, n_cols=w_in0.shape[1], rows_per_mod=seq, mod_row0=0,
        tm=2048, tn=768, name="even_in", cast_weights=[(even_w_out, 0, None)] + ffn_casts(0))
    ctx_kv, = _norm_matmul(ctx2d, g_mix, mods, 0, w_in0, col0=NA_WIDTH, n_cols=2 * NA_WIDTH,
                           rows_per_mod=batch * ctx_len, mod_row0=batch, tm=1024, tn=1024, name="ctx_kv")
    bias, cap = _bias_tables(rpb[0])
    attn, w_in1, w_out1, wg1, wu1, wd1 = _attention(
        proj, ctx_kv, bias, cap, batch, seq, ctx_len,
        cast_weights=[(odd_w_in, 0, None), (odd_w_out, 0, None)] + ffn_casts(1))
    x1, h = _even_out(attn, proj, conv_w[0], w_out0, x2d, mods, g_ffn, 0, seq, tm=512)
    x2, h = _ffn(h, wg0, wu0, wd0, x1, mods, 0, g_mix, 1, seq, tm=1024)

    u, v = _odd_in(h, w_in1, sg_ln_g[0][None], sg_ln_b[0][None], tm=512)
    b_s = jnp.broadcast_to(sg_b[0][:, :, None], (SG_GROUPS, CHUNK, CHUNK))
    x1, h = _odd_out(u, v, bf(sg_w[0]), b_s, w_out1, x2, mods, g_ffn, 1, seq, tm=512)
    out = _ffn(h, wg1, wu1, wd1, x1, mods, 1, final_norm_g[None], None, seq, tm=1024)
    return out.reshape(batch, seq, d)
```

```python
import functools

import jax
import jax.numpy as jnp
from jax import lax
from jax.experimental import pallas as pl
from jax.experimental.pallas import tpu as pltpu

D_MODEL = 2048
GRID_W = 64
NA_HEADS = 8
HEAD_DIM = 128
NA_WIDTH = NA_HEADS * HEAD_DIM
CONV_CH = D_MODEL - NA_WIDTH
WIN_ROWS = 8
WIN_COLS = 16
CHUNK = 128
SG_GROUPS = 16
EPS = 1e-6
NEG_INF = -1e30
UNMASKED_CAP = 3e38
MOD_ROWS = 16
ATTN_HEADS_PER_STEP = 2
NORM_SLICES = 4
FFN_TILE = 512
FFN_ROW_CHUNKS = 4
FFN_EPILOGUE_SLICES = 8
ROW_CHUNK = 256

F32 = jnp.float32
BF16 = jnp.bfloat16
BF16_SUBLANES = 16

MIB = 1024 * 1024
V7X_VMEM_BYTES = 64 * MIB
VMEM_CAP_BYTES = V7X_VMEM_BYTES - 5 * MIB
VMEM_SLACK_BYTES = 2 * MIB


def _nbytes(shape, dtype):
    n = jnp.dtype(dtype).itemsize
    for d in shape:
        n *= 1 if d is None else d
    return n


def _pallas(kernel_fn, *, name, grid, in_specs, args, out_specs, out_shape, scratch_shapes=(), temp_bytes):
    outs = list(zip(out_specs, out_shape)) if isinstance(out_specs, (list, tuple)) else [(out_specs, out_shape)]
    total = temp_bytes + VMEM_SLACK_BYTES
    for spec, arg in zip(in_specs, args):
        if spec.block_shape is not None:
            n_buffers = 2 if spec.pipeline_mode is None else spec.pipeline_mode.buffer_count
            total += n_buffers * _nbytes(spec.block_shape, arg.dtype)
    for spec, shape in outs:
        total += 2 * _nbytes(spec.block_shape, shape.dtype)
    for s in scratch_shapes:
        total += _nbytes(s.shape, s.dtype)
    assert total <= VMEM_CAP_BYTES, (name, total)
    return pl.pallas_call(
        kernel_fn, grid=grid, in_specs=in_specs, out_specs=out_specs, out_shape=out_shape,
        scratch_shapes=list(scratch_shapes),
        compiler_params=pltpu.CompilerParams(dimension_semantics=("arbitrary",) * len(grid),
                                             vmem_limit_bytes=total),
        name=name,
    )(*args)


def _resident_spec(shape):
    zeros = (0,) * len(shape)
    return pl.BlockSpec(shape, lambda *_: zeros, pipeline_mode=pl.Buffered(1))


def _dot(a, b):
    return jnp.dot(a, b, preferred_element_type=F32)


def _norm_mod(x, g, shift, scale):
    y = x * lax.rsqrt(jnp.mean(x * x, axis=-1, keepdims=True) + EPS)
    return (y * g) * (1.0 + scale) + shift


def _linear_step(ids, grid):
    step = ids[0]
    for axis in range(1, len(grid)):
        step = step * grid[axis] + ids[axis]
    return step


class _Cast:
    def __init__(self, stacked_w, layer, grid, col_tile=None):
        _, rows, cols = stacked_w.shape
        n_steps = 1
        for g in grid:
            n_steps *= g
        self.w = stacked_w
        self.grid = grid
        self.col_tile = col_tile
        self.n_blocks = nb = max(n for n in range(1, n_steps + 1)
                                 if rows % n == 0 and (rows // n) % BF16_SUBLANES == 0)
        rb = rows // nb
        block = lambda *ids: jnp.minimum(_linear_step(ids, grid), nb - 1)
        self.in_spec = pl.BlockSpec((None, rb, cols), lambda *ids: (layer, block(*ids), 0))
        if col_tile is None:
            self.out_shape = jax.ShapeDtypeStruct((rows, cols), BF16)
            self.out_spec = pl.BlockSpec((rb, cols), lambda *ids: (block(*ids), 0))
        else:
            nt = cols // col_tile
            self.out_shape = jax.ShapeDtypeStruct((nt, rows, col_tile), BF16)
            self.out_spec = pl.BlockSpec((nt, rb, col_tile), lambda *ids: (0, block(*ids), 0))

    def run(self, w_ref, o_ref):
        step = _linear_step([pl.program_id(a) for a in range(len(self.grid))], self.grid)

        @pl.when(step < self.n_blocks)
        def _():
            if self.col_tile is None:
                o_ref[...] = w_ref[...].astype(BF16)
            else:
                for t in range(o_ref.shape[0]):
                    o_ref[t] = w_ref[:, t * self.col_tile:(t + 1) * self.col_tile].astype(BF16)


def _with_casts(body, n_in, n_out, casts):
    n_c = len(casts)

    def kern(*refs):
        ins, rest = refs[:n_in], refs[n_in:]
        cast_ins, rest = rest[:n_c], rest[n_c:]
        outs, rest = rest[:n_out], rest[n_out:]
        cast_outs, scratch = rest[:n_c], rest[n_c:]
        body(*ins, *outs, *scratch)
        for cast, w_ref, o_ref in zip(casts, cast_ins, cast_outs):
            cast.run(w_ref, o_ref)

    return kern


def _ada_kernel(cond_ref, w_ref, b_ref, o_ref):
    a = jax.nn.silu(cond_ref[...]).astype(BF16)
    o_ref[...] = _dot(a, w_ref[...].astype(BF16)) + b_ref[...]


def _ada_mods(cond, ada_w, ada_b):
    depth, _, n = ada_w.shape
    tn = 1024
    return _pallas(
        _ada_kernel,
        name="ada_mods",
        grid=(depth, n // tn),
        in_specs=[
            pl.BlockSpec((MOD_ROWS, D_MODEL), lambda l, j: (0, 0)),
            pl.BlockSpec((None, D_MODEL, tn), lambda l, j: (l, 0, j)),
            pl.BlockSpec((None, 1, tn), lambda l, j: (l, 0, j)),
        ],
        args=(cond, ada_w, ada_b.reshape(depth, 1, n)),
        out_specs=pl.BlockSpec((None, MOD_ROWS, tn), lambda l, j: (l, 0, j)),
        out_shape=jax.ShapeDtypeStruct((depth, MOD_ROWS, n), F32),
        temp_bytes=_nbytes((D_MODEL, tn), BF16),
    )


def _mod_spec(layer, chunk, n_grid_axes):
    if n_grid_axes == 1:
        return pl.BlockSpec((None, MOD_ROWS, D_MODEL), lambda i: (layer, 0, chunk))
    return pl.BlockSpec((None, MOD_ROWS, D_MODEL), lambda i, j: (layer, 0, chunk))


def _norm_matmul_kernel(x_ref, g_ref, sh_ref, sc_ref, w_ref, o_ref, h_scr, *,
                        n_row_tiles, n_slices, tiles_per_mod, mod_row0):
    i = pl.program_id(0)
    j = pl.program_id(1)
    rs = x_ref.shape[0]

    def norm_slice():
        r0 = pl.multiple_of(jnp.minimum(j, n_slices - 1) * rs, rs)
        row = mod_row0 + i // tiles_per_mod
        h = _norm_mod(x_ref[...], g_ref[...], sh_ref[pl.ds(row, 1), :], sc_ref[pl.ds(row, 1), :])
        h_scr[i % 2, pl.ds(r0, rs), :] = h.astype(BF16)

    def matmul():
        o_ref[...] = _dot(h_scr[(i - 1) % 2], w_ref[...]).astype(o_ref.dtype)

    @pl.when(i == 0)
    def _():
        norm_slice()

    @pl.when((i > 0) & (i < n_row_tiles))
    def _():
        matmul()
        norm_slice()

    @pl.when(i == n_row_tiles)
    def _():
        matmul()


def _norm_matmul(x2d, g, mods, layer, w, *, col0, n_cols, rows_per_mod, mod_row0, tm, tn, name,
                 cast_weights=()):
    m = x2d.shape[0]
    n_i, n_j = m // tm, n_cols // tn
    n_s = min(n_j, NORM_SLICES)
    grid = (n_i + 1, n_j)
    body = functools.partial(_norm_matmul_kernel, n_row_tiles=n_i, n_slices=n_s,
                             tiles_per_mod=max(rows_per_mod // tm, 1), mod_row0=mod_row0)
    casts = [_Cast(cw, layer_, grid, col_tile) for cw, layer_, col_tile in cast_weights]
    cb0 = col0 // tn
    x_block = lambda i, j: jnp.minimum(i, n_i - 1) * n_s + jnp.minimum(j, n_s - 1)
    w_block = lambda i, j: cb0 + jnp.where(i == 0, 0, j)
    return _pallas(
        _with_casts(body, 5, 1, casts),
        name=name,
        grid=grid,
        in_specs=[
            pl.BlockSpec((tm // n_s, D_MODEL), lambda i, j: (x_block(i, j), 0)),
            pl.BlockSpec((None, 1, D_MODEL), lambda i, j: (layer, 0, 0)),
            _mod_spec(layer, 0, 2),
            _mod_spec(layer, 1, 2),
            pl.BlockSpec((D_MODEL, tn), lambda i, j: (0, w_block(i, j))),
        ] + [c.in_spec for c in casts],
        args=(x2d, g, mods, mods, w, *[c.w for c in casts]),
        out_specs=[pl.BlockSpec((tm, tn), lambda i, j: (jnp.maximum(i - 1, 0), jnp.where(i == 0, 0, j)))]
        + [c.out_spec for c in casts],
        out_shape=[jax.ShapeDtypeStruct((m, n_cols), BF16)] + [c.out_shape for c in casts],
        scratch_shapes=[pltpu.VMEM((2, tm, D_MODEL), BF16)],
        temp_bytes=_nbytes((tm, tn), F32) + 3 * _nbytes((tm // n_s, D_MODEL), F32),
    )


def _bias_kernel(rpb_ref, bias_ref, cap_ref):
    h = pl.program_id(0)
    n_dc = 2 * WIN_COLS - 1
    n_dr = 2 * WIN_ROWS - 1
    q = lax.broadcasted_iota(jnp.int32, (GRID_W, 2 * GRID_W), 0)
    lane = lax.broadcasted_iota(jnp.int32, (GRID_W, 2 * GRID_W), 1)
    second = lane >= GRID_W
    k = jnp.where(second, lane - GRID_W, lane)
    dc = jnp.clip(k - q, -(WIN_COLS - 1), WIN_COLS - 1) + (WIN_COLS - 1)
    pairs = []
    for m in range(n_dr - 1):
        acc = jnp.zeros((GRID_W, 2 * GRID_W), F32)
        for d in range(n_dc):
            val = jnp.where(second, rpb_ref[h, (m + 1) * n_dc + d], rpb_ref[h, m * n_dc + d])
            acc = jnp.where(dc == d, val, acc)
        pairs.append(acc)
    for o in range(WIN_ROWS):
        for jj in range(WIN_ROWS // 2):
            bias_ref[o, :, jj * 2 * GRID_W:(jj + 1) * 2 * GRID_W] = pairs[o + 2 * jj]
    col_start = jnp.clip(q - WIN_COLS // 2, 0, GRID_W - WIN_COLS)
    cap = jnp.where((k >= col_start) & (k < col_start + WIN_COLS), UNMASKED_CAP, NEG_INF).astype(F32)
    for jj in range(WIN_ROWS // 2):
        cap_ref[:, jj * 2 * GRID_W:(jj + 1) * 2 * GRID_W] = cap


def _bias_tables(rpb):
    h = rpb.shape[0]
    kw = WIN_ROWS * GRID_W
    return _pallas(
        _bias_kernel,
        name="bias_tables",
        grid=(h,),
        in_specs=[pl.BlockSpec(memory_space=pltpu.SMEM)],
        args=(rpb.reshape(h, -1),),
        out_specs=[
            pl.BlockSpec((None, WIN_ROWS, GRID_W, kw), lambda i: (i, 0, 0, 0)),
            pl.BlockSpec((GRID_W, kw), lambda i: (0, 0)),
        ],
        out_shape=[
            jax.ShapeDtypeStruct((h, WIN_ROWS, GRID_W, kw), F32),
            jax.ShapeDtypeStruct((GRID_W, kw), F32),
        ],
        temp_bytes=2 * WIN_ROWS * _nbytes((GRID_W, 2 * GRID_W), F32),
    )


def _attn_kernel(q_ref, k_ref, v_ref, kc_ref, vc_ref, bias_ref, cap_ref, o_ref, *scratch, rows):
    scale = HEAD_DIM ** -0.5
    kw = WIN_ROWS * GRID_W
    ctx_len = kc_ref.shape[0]
    nt = (((1,), (1,)), ((), ()))
    per_head = len(scratch) // ATTN_HEADS_PER_STEP

    def parts(head):
        return scratch[head * per_head:(head + 1) * per_head], slice(head * HEAD_DIM, (head + 1) * HEAD_DIM)

    def setup(head):
        (s_scr, _, vx_scr, vcx_scr, _), lanes = parts(head)
        vx_scr[:, :HEAD_DIM] = v_ref[:, lanes]
        vx_scr[:, HEAD_DIM:] = jnp.ones((v_ref.shape[0], HEAD_DIM), BF16)
        vcx_scr[:, :HEAD_DIM] = vc_ref[:, lanes]
        vcx_scr[:, HEAD_DIM:] = jnp.ones((ctx_len, HEAD_DIM), BF16)
        s_scr[:, kw:] = lax.dot_general(q_ref[:, lanes], kc_ref[:, lanes], nt,
                                        preferred_element_type=F32) * scale

    def scores(head, r):
        (s_scr, _, _, _, _), lanes = parts(head)
        row_start = jnp.clip(r - WIN_ROWS // 2, 0, rows - WIN_ROWS)
        o = row_start - r + (WIN_ROWS - 1)
        q0 = pl.multiple_of(r * GRID_W, GRID_W)
        k0 = pl.multiple_of(row_start * GRID_W, GRID_W)
        s = lax.dot_general(q_ref[pl.ds(q0, GRID_W), lanes], k_ref[pl.ds(k0, kw), lanes], nt,
                            preferred_element_type=F32)
        s_scr[pl.ds(q0, GRID_W), :kw] = jnp.minimum(s * scale + bias_ref[head, o], cap_ref[...])

    def expo(head, r):
        (s_scr, e_scr, _, _, _), _ = parts(head)
        q0 = pl.multiple_of(r * GRID_W, GRID_W)
        s = s_scr[pl.ds(q0, GRID_W), :]
        e_scr[pl.ds(q0, GRID_W), :] = jnp.exp(s - jnp.max(s, axis=-1, keepdims=True)).astype(BF16)

    def ctx_values(head):
        (_, e_scr, _, vcx_scr, acc_scr), _ = parts(head)
        acc_scr[...] = _dot(e_scr[:, kw:], vcx_scr[...])

    def values(head, r):
        (_, e_scr, vx_scr, _, acc_scr), lanes = parts(head)
        row_start = jnp.clip(r - WIN_ROWS // 2, 0, rows - WIN_ROWS)
        q0 = pl.multiple_of(r * GRID_W, GRID_W)
        k0 = pl.multiple_of(row_start * GRID_W, GRID_W)
        t = acc_scr[pl.ds(q0, GRID_W), :] + _dot(e_scr[pl.ds(q0, GRID_W), :kw], vx_scr[pl.ds(k0, kw), :])
        o_ref[pl.ds(q0, GRID_W), lanes] = (t[:, :HEAD_DIM] / t[:, HEAD_DIM:]).astype(o_ref.dtype)

    def loop(*row_fns):
        def body(r, carry):
            for fn in row_fns:
                fn(r)
            return carry
        lax.fori_loop(0, rows, body, 0, unroll=16)

    setup(0)
    setup(1)
    loop(functools.partial(scores, 0))
    loop(functools.partial(expo, 0), functools.partial(scores, 1))
    ctx_values(0)
    loop(functools.partial(expo, 1), functools.partial(values, 0))
    ctx_values(1)
    loop(functools.partial(values, 1))


def _attention(qkv, ctx_kv, bias, cap, batch, seq, ctx_len, cast_weights=()):
    rows = seq // GRID_W
    kw = WIN_ROWS * GRID_W
    hps = ATTN_HEADS_PER_STEP
    n_hg = NA_HEADS // hps
    grid = (n_hg, batch)
    casts = [_Cast(cw, layer_, grid, col_tile) for cw, layer_, col_tile in cast_weights]
    return _pallas(
        _with_casts(functools.partial(_attn_kernel, rows=rows), 7, 1, casts),
        name="nbr_attention",
        grid=grid,
        in_specs=[
            pl.BlockSpec((seq, hps * HEAD_DIM), lambda h, b: (b, h)),
            pl.BlockSpec((seq, hps * HEAD_DIM), lambda h, b: (b, n_hg + h)),
            pl.BlockSpec((seq, hps * HEAD_DIM), lambda h, b: (b, 2 * n_hg + h)),
            pl.BlockSpec((ctx_len, hps * HEAD_DIM), lambda h, b: (b, h)),
            pl.BlockSpec((ctx_len, hps * HEAD_DIM), lambda h, b: (b, n_hg + h)),
            pl.BlockSpec((hps, WIN_ROWS, GRID_W, kw), lambda h, b: (h, 0, 0, 0)),
            pl.BlockSpec((GRID_W, kw), lambda h, b: (0, 0)),
        ] + [c.in_spec for c in casts],
        args=(qkv, qkv, qkv, ctx_kv, ctx_kv, bias, cap, *[c.w for c in casts]),
        out_specs=[pl.BlockSpec((seq, hps * HEAD_DIM), lambda h, b: (b, h))] + [c.out_spec for c in casts],
        out_shape=[jax.ShapeDtypeStruct((batch * seq, NA_WIDTH), BF16)] + [c.out_shape for c in casts],
        scratch_shapes=[
            pltpu.VMEM((seq, kw + ctx_len), F32),
            pltpu.VMEM((seq, kw + ctx_len), BF16),
            pltpu.VMEM((seq, 2 * HEAD_DIM), BF16),
            pltpu.VMEM((ctx_len, 2 * HEAD_DIM), BF16),
            pltpu.VMEM((seq, 2 * HEAD_DIM), F32),
        ] * hps,
        temp_bytes=_nbytes((seq, ctx_len), F32),
    )


def _residual_and_ffn_norm(rows, mod_row, x_ref, mix, gate_ref, gn_ref, shf_ref, scf_ref, x1_ref, h_ref):
    x1 = x_ref[rows, :] + gate_ref[pl.ds(mod_row, 1), :] * mix
    x1_ref[rows, :] = x1
    h = _norm_mod(x1, gn_ref[...], shf_ref[pl.ds(mod_row, 1), :], scf_ref[pl.ds(mod_row, 1), :])
    h_ref[rows, :] = h.astype(BF16)


def _row_chunks(tm):
    return [slice(r, r + ROW_CHUNK) for r in range(0, tm, ROW_CHUNK)]


def _skewed(chunks, matmul_fn, epilogue_fn):
    pending = None
    for rows in chunks:
        acc = matmul_fn(rows)
        if pending is not None:
            epilogue_fn(*pending)
        pending = (rows, acc)
    epilogue_fn(*pending)


def _even_out_kernel(attn_ref, b_ref, c_ref, u_ref, cp_ref, up_ref, cn_ref, un_ref, cw_ref, wo_ref,
                     x_ref, gate_ref, gn_ref, shf_ref, scf_ref, x1_ref, h_ref, conv_scr, *, tiles_per_mod, halo):
    i = pl.program_id(0)
    tm = c_ref.shape[0]
    t = i % tiles_per_mod
    conv_scr[:, :NA_WIDTH] = attn_ref[...]
    z = c_ref[...].astype(F32) * u_ref[...].astype(F32)
    z_before = cp_ref[halo - 1:halo, :].astype(F32) * up_ref[halo - 1:halo, :].astype(F32)
    z_after = cn_ref[0:1, :].astype(F32) * un_ref[0:1, :].astype(F32)
    z_before = jnp.where(t == 0, 0.0, z_before)
    z_after = jnp.where(t == tiles_per_mod - 1, 0.0, z_after)
    ridx = lax.broadcasted_iota(jnp.int32, (tm, 1), 0)
    z_prev = jnp.where(ridx == 0, z_before, pltpu.roll(z, 1, 0))
    z_next = jnp.where(ridx == tm - 1, z_after, pltpu.roll(z, tm - 1, 0))
    y = z_prev * cw_ref[0:1, :] + z * cw_ref[1:2, :] + z_next * cw_ref[2:3, :]
    conv_scr[:, NA_WIDTH:] = (b_ref[...].astype(F32) * y).astype(BF16)
    _skewed(
        _row_chunks(tm),
        lambda rows: _dot(conv_scr[rows, :], wo_ref[...]),
        lambda rows, mix: _residual_and_ffn_norm(rows, i // tiles_per_mod, x_ref, mix, gate_ref, gn_ref,
                                                 shf_ref, scf_ref, x1_ref, h_ref))


def _even_out(attn, proj, conv_w, w_out, x2d, mods, norm_ffn_g, layer, seq, tm):
    m = x2d.shape[0]
    halo = 16
    hb = tm // halo
    n_halo_blocks = m // halo
    cb = NA_WIDTH * 3 // CONV_CH
    kern = functools.partial(_even_out_kernel, tiles_per_mod=seq // tm, halo=halo)
    main = lambda c: pl.BlockSpec((tm, CONV_CH), lambda i: (i, c))
    prev = lambda c: pl.BlockSpec((halo, CONV_CH), lambda i: (jnp.maximum(i * hb - 1, 0), c))
    nxt = lambda c: pl.BlockSpec((halo, CONV_CH), lambda i: (jnp.minimum((i + 1) * hb, n_halo_blocks - 1), c))
    return _pallas(
        kern,
        name="even_out",
        grid=(m // tm,),
        in_specs=[
            pl.BlockSpec((tm, NA_WIDTH), lambda i: (i, 0)),
            main(cb), main(cb + 1), main(cb + 2),
            prev(cb + 1), prev(cb + 2), nxt(cb + 1), nxt(cb + 2),
            _resident_spec(conv_w.shape),
            _resident_spec(w_out.shape),
            pl.BlockSpec((tm, D_MODEL), lambda i: (i, 0)),
            _mod_spec(layer, 2, 1),
            pl.BlockSpec((None, 1, D_MODEL), lambda i: (layer, 0, 0)),
            _mod_spec(layer, 3, 1),
            _mod_spec(layer, 4, 1),
        ],
        args=(attn, proj, proj, proj, proj, proj, proj, proj, conv_w, w_out, x2d, mods, norm_ffn_g, mods, mods),
        out_specs=[
            pl.BlockSpec((tm, D_MODEL), lambda i: (i, 0)),
            pl.BlockSpec((tm, D_MODEL), lambda i: (i, 0)),
        ],
        out_shape=[
            jax.ShapeDtypeStruct((m, D_MODEL), F32),
            jax.ShapeDtypeStruct((m, D_MODEL), BF16),
        ],
        scratch_shapes=[pltpu.VMEM((tm, NA_WIDTH + CONV_CH), BF16)],
        temp_bytes=3 * _nbytes((tm, CONV_CH), F32) + 2 * _nbytes((tm, D_MODEL), F32),
    )


def _ffn_kernel(*refs, n_row_tiles, tiles_per_mod, modulated):
    if modulated:
        (h_ref, wg_ref, wu_ref, wd_ref, x_ref, gate_ref, gn_ref, sh_ref, sc_ref,
         x2_ref, y_ref, acc_ref) = refs
    else:
        h_ref, wg_ref, wu_ref, wd_ref, x_ref, gate_ref, gn_ref, y_ref, acc_ref = refs
    i = pl.program_id(0)
    f = pl.program_id(1)
    slot = i % 2
    tm = h_ref.shape[0]
    rs = tm // FFN_EPILOGUE_SLICES

    def matmuls(first):
        rc = tm // FFN_ROW_CHUNKS
        ts = []
        for c in range(FFN_ROW_CHUNKS):
            h = h_ref[c * rc:(c + 1) * rc, :]
            ts.append((jax.nn.silu(_dot(h, wg_ref[...])) * _dot(h, wu_ref[...])).astype(BF16))
        for c in range(FFN_ROW_CHUNKS):
            part = _dot(ts[c], wd_ref[...])
            if first:
                acc_ref[slot, c * rc:(c + 1) * rc, :] = part
            else:
                acc_ref[slot, c * rc:(c + 1) * rc, :] += part

    def epilogue_slice():
        r0 = pl.multiple_of(f * rs, rs)
        row = (i - 1) // tiles_per_mod
        x2 = x_ref[...] + gate_ref[pl.ds(row, 1), :] * acc_ref[1 - slot, pl.ds(r0, rs), :]
        if modulated:
            x2_ref[...] = x2
            y = _norm_mod(x2, gn_ref[...], sh_ref[pl.ds(row, 1), :], sc_ref[pl.ds(row, 1), :])
        else:
            y = x2 * lax.rsqrt(jnp.mean(x2 * x2, axis=-1, keepdims=True) + EPS) * gn_ref[...]
        y_ref[...] = y.astype(y_ref.dtype)

    has_slice = (i > 0) & (f < FFN_EPILOGUE_SLICES)
    has_matmuls = i < n_row_tiles

    for first in (True, False):
        mine = (f == 0) if first else (f > 0)

        @pl.when(mine & has_matmuls & jnp.logical_not(has_slice))
        def _():
            matmuls(first)

        @pl.when(mine & has_matmuls & has_slice)
        def _():
            epilogue_slice()
            matmuls(first)

    @pl.when(jnp.logical_not(has_matmuls) & has_slice)
    def _():
        epilogue_slice()


def _ffn(h, w_gate, w_up, w_down, x2d, mods, layer, next_g, next_layer, seq, tm):
    m = x2d.shape[0]
    n_f, _, tf = w_gate.shape
    n_i = m // tm
    n_s = FFN_EPILOGUE_SLICES
    assert n_f >= n_s and (tm // n_s) % BF16_SUBLANES == 0
    modulated = next_layer is not None
    kern = functools.partial(_ffn_kernel, n_row_tiles=n_i, tiles_per_mod=seq // tm, modulated=modulated)
    w_tile = lambda i, f: jnp.where(i == n_i, n_f - 1, f)
    e_block = lambda i, f: jnp.where(i == 0, 0, (i - 1) * n_s + jnp.minimum(f, n_s - 1))
    slice_spec = pl.BlockSpec((tm // n_s, D_MODEL), lambda i, f: (e_block(i, f), 0))
    in_specs = [
        pl.BlockSpec((tm, D_MODEL), lambda i, f: (jnp.minimum(i, n_i - 1), 0)),
        pl.BlockSpec((None, D_MODEL, tf), lambda i, f: (w_tile(i, f), 0, 0)),
        pl.BlockSpec((None, D_MODEL, tf), lambda i, f: (w_tile(i, f), 0, 0)),
        pl.BlockSpec((tf, D_MODEL), lambda i, f: (w_tile(i, f), 0)),
        slice_spec,
        _mod_spec(layer, 5, 2),
    ]
    args = [h, w_gate, w_up, w_down, x2d, mods]
    if modulated:
        in_specs += [pl.BlockSpec((None, 1, D_MODEL), lambda i, f: (next_layer, 0, 0)),
                     _mod_spec(next_layer, 0, 2), _mod_spec(next_layer, 1, 2)]
        args += [next_g, mods, mods]
        out_specs = [slice_spec, slice_spec]
        out_shape = [jax.ShapeDtypeStruct((m, D_MODEL), F32), jax.ShapeDtypeStruct((m, D_MODEL), BF16)]
    else:
        in_specs += [pl.BlockSpec((1, D_MODEL), lambda i, f: (0, 0))]
        args += [next_g]
        out_specs = slice_spec
        out_shape = jax.ShapeDtypeStruct((m, D_MODEL), F32)
    rc = tm // FFN_ROW_CHUNKS
    return _pallas(
        kern,
        name="ffn_final" if not modulated else "ffn",
        grid=(n_i + 1, n_f),
        in_specs=in_specs,
        args=args,
        out_specs=out_specs,
        out_shape=out_shape,
        scratch_shapes=[pltpu.VMEM((2, tm, D_MODEL), F32)],
        temp_bytes=FFN_ROW_CHUNKS * (2 * _nbytes((rc, tf), F32) + _nbytes((rc, tf), BF16))
        + _nbytes((rc, D_MODEL), F32),
    )


def _odd_in_kernel(h_ref, w_ref, lg_ref, lb_ref, u_ref, v_ref):
    def matmuls(rows):
        h = h_ref[rows, :]
        return _dot(h, w_ref[:, :D_MODEL]), _dot(h, w_ref[:, D_MODEL:])

    def epilogue(rows, acc):
        u_ref[rows, :] = jax.nn.gelu(acc[0], approximate=True).astype(BF16)
        z = jax.nn.gelu(acc[1], approximate=True)
        zc = z - jnp.mean(z, axis=-1, keepdims=True)
        var = jnp.mean(zc * zc, axis=-1, keepdims=True)
        v_ref[rows, :] = (zc * lax.rsqrt(var + EPS) * lg_ref[...] + lb_ref[...]).astype(BF16)

    _skewed(_row_chunks(h_ref.shape[0]), matmuls, epilogue)


def _odd_in(h, w_in, ln_g, ln_b, tm):
    m = h.shape[0]
    row_spec = pl.BlockSpec((tm, D_MODEL), lambda i: (i, 0))
    return _pallas(
        _odd_in_kernel,
        name="odd_in",
        grid=(m // tm,),
        in_specs=[
            row_spec,
            _resident_spec(w_in.shape),
            pl.BlockSpec((1, D_MODEL), lambda i: (0, 0)),
            pl.BlockSpec((1, D_MODEL), lambda i: (0, 0)),
        ],
        args=(h, w_in, ln_g, ln_b),
        out_specs=[row_spec, row_spec],
        out_shape=[jax.ShapeDtypeStruct((m, D_MODEL), BF16)] * 2,
        temp_bytes=2 * 4 * _nbytes((ROW_CHUNK, D_MODEL), F32),
    )


def _odd_out_kernel(u_ref, v_ref, ws_ref, bs_ref, wo_ref, x_ref, gate_ref, gn_ref, shf_ref, scf_ref,
                    x1_ref, h_ref, t_scr, *, tiles_per_mod):
    i = pl.program_id(0)
    n_chunks = u_ref.shape[0] // CHUNK
    for g in range(SG_GROUPS):
        cols = slice(g * CHUNK, (g + 1) * CHUNK)
        vcat = jnp.concatenate([v_ref[n * CHUNK:(n + 1) * CHUNK, cols] for n in range(n_chunks)], axis=1)
        mixed = _dot(ws_ref[g], vcat)
        for n in range(n_chunks):
            rows = slice(n * CHUNK, (n + 1) * CHUNK)
            gated = u_ref[rows, cols].astype(F32) * (mixed[:, n * CHUNK:(n + 1) * CHUNK] + bs_ref[g])
            t_scr[rows, cols] = gated.astype(BF16)
    _skewed(
        _row_chunks(u_ref.shape[0]),
        lambda rows: _dot(t_scr[rows, :], wo_ref[...]),
        lambda rows, mix: _residual_and_ffn_norm(rows, i // tiles_per_mod, x_ref, mix, gate_ref, gn_ref,
                                                 shf_ref, scf_ref, x1_ref, h_ref))


def _odd_out(u, v, w_s, b_s, w_out, x2d, mods, norm_ffn_g, layer, seq, tm):
    m = x2d.shape[0]
    kern = functools.partial(_odd_out_kernel, tiles_per_mod=seq // tm)
    return _pallas(
        kern,
        name="odd_out",
        grid=(m // tm,),
        in_specs=[
            pl.BlockSpec((tm, D_MODEL), lambda i: (i, 0)),
            pl.BlockSpec((tm, D_MODEL), lambda i: (i, 0)),
            _resident_spec(w_s.shape),
            _resident_spec(b_s.shape),
            _resident_spec(w_out.shape),
            pl.BlockSpec((tm, D_MODEL), lambda i: (i, 0)),
            _mod_spec(layer, 2, 1),
            pl.BlockSpec((None, 1, D_MODEL), lambda i: (layer, 0, 0)),
            _mod_spec(layer, 3, 1),
            _mod_spec(layer, 4, 1),
        ],
        args=(u, v, w_s, b_s, w_out, x2d, mods, norm_ffn_g, mods, mods),
        out_specs=[
            pl.BlockSpec((tm, D_MODEL), lambda i: (i, 0)),
            pl.BlockSpec((tm, D_MODEL), lambda i: (i, 0)),
        ],
        out_shape=[
            jax.ShapeDtypeStruct((m, D_MODEL), F32),
            jax.ShapeDtypeStruct((m, D_MODEL), BF16),
        ],
        scratch_shapes=[pltpu.VMEM((tm, D_MODEL), BF16)],
        temp_bytes=2 * 2 * _nbytes((ROW_CHUNK, D_MODEL), F32),
    )


def kernel(x, c, ctx, c_ctx, ada_w, ada_b, norm_mix_g, norm_ffn_g, ffn_w_gate, ffn_w_up, ffn_w_down,
           even_w_in, even_w_out, rpb, conv_w, odd_w_in, odd_w_out, sg_ln_g, sg_ln_b, sg_w, sg_b,
           final_norm_g):
    batch, seq, d = x.shape
    ctx_len = ctx.shape[1]
    depth = ada_w.shape[0]
    assert d == D_MODEL and depth == 2 and batch < MOD_ROWS and seq % (GRID_W * WIN_ROWS) == 0

    bf = lambda w: w.astype(BF16)
    x2d = x.reshape(batch * seq, d)
    ctx2d = ctx.reshape(batch * ctx_len, d)
    g_mix = norm_mix_g.reshape(depth, 1, d)
    g_ffn = norm_ffn_g.reshape(depth, 1, d)

    cond = jnp.concatenate([c, c_ctx[None], jnp.zeros((MOD_ROWS - batch - 1, d), F32)], axis=0)
    mods = _ada_mods(cond, ada_w, ada_b)

    ffn_casts = lambda layer: [(ffn_w_gate, layer, FFN_TILE), (ffn_w_up, layer, FFN_TILE),
                               (ffn_w_down, layer, None)]
    w_in0 = bf(even_w_in[0])
    proj, w_out0, wg0, wu0, wd0 = _norm_matmul(
        x2d, g_mix, mods, 0, w_in0, col0=0, n_cols=w_in0.shape[1], rows_per_mod=seq, mod_row0=0,
        tm=1024, tn=1536, name="even_in", cast_weights=[(even_w_out, 0, None)] + ffn_casts(0))
    ctx_kv, = _norm_matmul(ctx2d, g_mix, mods, 0, w_in0, col0=NA_WIDTH, n_cols=2 * NA_WIDTH,
                           rows_per_mod=batch * ctx_len, mod_row0=batch, tm=1024, tn=1024, name="ctx_kv")
    bias, cap = _bias_tables(rpb[0])
    attn, w_in1, w_out1, wg1, wu1, wd1 = _attention(
        proj, ctx_kv, bias, cap, batch, seq, ctx_len,
        cast_weights=[(odd_w_in, 0, None), (odd_w_out, 0, None)] + ffn_casts(1))
    x1, h = _even_out(attn, proj, conv_w[0], w_out0, x2d, mods, g_ffn, 0, seq, tm=512)
    x2, h = _ffn(h, wg0, wu0, wd0, x1, mods, 0, g_mix, 1, seq, tm=1024)

    u, v = _odd_in(h, w_in1, sg_ln_g[0][None], sg_ln_b[0][None], tm=512)
    b_s = jnp.broadcast_to(sg_b[0][:, :, None], (SG_GROUPS, CHUNK, CHUNK))
    x1, h = _odd_out(u, v, bf(sg_w[0]), b_s, w_out1, x2, mods, g_ffn, 1, seq, tm=512)
    out = _ffn(h, wg1, wu1, wd1, x1, mods, 1, final_norm_g[None], None, seq, tm=1024)
    return out.reshape(batch, seq, d)
```
